```python
import jax, jax.numpy as jnp
from jax import lax
import numpy as np

D_MODEL = 1024
BATCH = 8
SEQ = 2048
DEPTH = 2
DEC_BATCH = 16
DEC_SEQ = 64
PAST_LEN = 4096

CHUNK = 64
N_MIXERS = 2
N_A_LAYERS = (DEPTH + N_MIXERS - 1) // N_MIXERS
N_B_LAYERS = DEPTH // N_MIXERS
HG_DK = 128
HG_HEADS = D_MODEL // HG_DK
HG_DV = D_MODEL // HG_HEADS
HG_KDIM = HG_HEADS * HG_DK
HG_VDIM = HG_HEADS * HG_DV
SB_HEADS = 16
SB_HEAD_DIM = D_MODEL // SB_HEADS
QBLOCK = 128
D_FF = 2816
CONV_WIDTH = 3
EPS = 1e-6

kernel_name = "hgrn2_stickbreaking_convffn_stream_step"


def rmsnorm(x, g):
    xf = x.astype(jnp.float32)
    r = lax.rsqrt(jnp.mean(xf * xf, axis=-1, keepdims=True) + EPS)
    return (xf * r * g.astype(jnp.float32)).astype(x.dtype)


def hgrn2_scan(q, logf, k, v, s0):
    B, T, H, DK = q.shape
    DV = v.shape[-1]
    C = CHUNK if T % CHUNK == 0 else T
    n = T // C

    def to_chunks(a):
        return a.reshape(B, n, C, H, a.shape[-1]).transpose(1, 0, 2, 3, 4)

    mask = jnp.tril(jnp.ones((C, C), dtype=bool))

    def step(S, inp):
        qc, gc, kc, vc = inp
        b = jnp.cumsum(gc, axis=1)
        b_last = b[:, -1]
        q_t = qc * jnp.exp(b)
        k_t = kc * jnp.exp(-b)
        attn = jnp.einsum('bthk,bshk->bhts', q_t, k_t)
        attn = jnp.where(mask[None, None], attn, 0.0)
        o = jnp.einsum('bhts,bshv->bthv', attn, vc) + jnp.einsum('bthk,bhkv->bthv', q_t, S)
        k_end = kc * jnp.exp(b_last[:, None] - b)
        S_new = jnp.exp(b_last)[..., None] * S + jnp.einsum('bshk,bshv->bhkv', k_end, vc)
        return S_new, o

    S_fin, o = lax.scan(step, s0, (to_chunks(q), to_chunks(logf), to_chunks(k), to_chunks(v)))
    o = o.transpose(1, 0, 2, 3, 4).reshape(B, T, H, DV)
    return o, S_fin


def hgrn2_mixer(h, w_in, w_out, gn, lb, s0):
    B, T, _ = h.shape
    f32 = jnp.float32
    proj = h @ w_in
    q = proj[..., :HG_KDIM]
    fpre = proj[..., HG_KDIM:2 * HG_KDIM].astype(f32)
    i_in = proj[..., 2 * HG_KDIM:2 * HG_KDIM + HG_VDIM]
    gate = proj[..., 2 * HG_KDIM + HG_VDIM:]
    lb_f = lb.reshape(HG_KDIM)
    f = lb_f + (1.0 - lb_f) * jax.nn.sigmoid(fpre)
    logf = jnp.log(f)
    k = (1.0 - lb_f) * jax.nn.sigmoid(-fpre)
    hd = lambda a, d: a.astype(f32).reshape(B, T, HG_HEADS, d)
    o, s_fin = hgrn2_scan(hd(q, HG_DK), hd(logf, HG_DK), hd(k, HG_DK), hd(i_in, HG_DV),
                          s0.astype(f32))
    o = o * lax.rsqrt(jnp.mean(o * o, axis=-1, keepdims=True) + EPS) * gn.astype(f32)
    o = o.reshape(B, T, HG_VDIM) * jax.nn.sigmoid(gate.astype(f32))
    out = o.astype(h.dtype) @ w_out
    return out, s_fin.astype(h.dtype)


def sb_attend(q, k, v, q_start):
    f32 = jnp.float32
    B, Tq, H, d = q.shape
    scale = 1.0 / float(np.sqrt(d))
    outs = []
    for b0 in range(0, Tq, QBLOCK):
        b1 = min(b0 + QBLOCK, Tq)
        kend = q_start + b1
        qb = q[:, b0:b1].astype(f32)
        kb = k[:, :kend].astype(f32)
        vb = v[:, :kend].astype(f32)
        z = jnp.einsum('bqhd,bkhd->bhqk', qb, kb) * scale
        qpos = q_start + jnp.arange(b0, b1)
        kpos = jnp.arange(kend)
        mask = (kpos[None, :] < qpos[:, None])[None, None]
        log_1mb = jnp.where(mask, -jax.nn.softplus(z), 0.0)
        rev = lax.cumsum(log_1mb, axis=3, reverse=True)
        excl = jnp.concatenate([rev[..., 1:], jnp.zeros_like(rev[..., :1])], axis=-1)
        A = jnp.where(mask, jnp.exp(-jax.nn.softplus(-z) + excl), 0.0)
        outs.append(jnp.einsum('bhqk,bkhd->bqhd', A, vb))
    return jnp.concatenate(outs, axis=1).astype(q.dtype)


def stick_breaking_mixer(h, w_in, w_out, k_past, v_past):
    B, T, _ = h.shape
    proj = h @ w_in
    q, k, v = jnp.split(proj, 3, axis=-1)
    hd = lambda a: a.reshape(B, T, SB_HEADS, SB_HEAD_DIM)
    q, k, v = hd(q), hd(k), hd(v)
    past = k_past.shape[1]
    k_all = jnp.concatenate([k_past.astype(h.dtype), k], axis=1)
    v_all = jnp.concatenate([v_past.astype(h.dtype), v], axis=1)
    o = sb_attend(q, k_all, v_all, past)
    out = o.reshape(B, T, D_MODEL) @ w_out
    return out, k, v


def conv_ffn(h, w_up, conv_w, conv_b, w_down, s0):
    T = h.shape[1]
    a, g = jnp.split(h @ w_up, 2, axis=-1)
    full = jnp.concatenate([s0.astype(a.dtype), a], axis=1)
    c = conv_b + sum(full[:, j:j + T] * conv_w[j] for j in range(CONV_WIDTH))
    out = (jax.nn.gelu(c) * g) @ w_down
    return out, full[:, -(CONV_WIDTH - 1):]


def trunk(x, hg_s0, k_past, v_past, conv_s0, norm_mix, norm_ffn, norm_final,
          w_in_a, w_out_a, gn_a, lb_logits, w_in_b, w_out_b, w_up, conv_w, conv_b, w_down):
    lb_all = jnp.cumsum(jax.nn.softmax(lb_logits.astype(jnp.float32), axis=0), axis=0)
    hg_states, k_rows, v_rows, conv_states = [], [], [], []
    for i in range(DEPTH):
        j = i // N_MIXERS
        h = rmsnorm(x, norm_mix[i])
        if i % N_MIXERS == 0:
            m, s_fin = hgrn2_mixer(h, w_in_a[j], w_out_a[j], gn_a[j], lb_all[i], hg_s0[j])
            hg_states.append(s_fin)
        else:
            m, k_new, v_new = stick_breaking_mixer(h, w_in_b[j], w_out_b[j], k_past[j], v_past[j])
            k_rows.append(k_new)
            v_rows.append(v_new)
        x = x + m
        f, c_state = conv_ffn(rmsnorm(x, norm_ffn[i]), w_up[i], conv_w[i], conv_b[i], w_down[i], conv_s0[i])
        conv_states.append(c_state)
        x = x + f
    y = rmsnorm(x, norm_final)
    return y, jnp.stack(hg_states), jnp.stack(k_rows), jnp.stack(v_rows), jnp.stack(conv_states)


def setup_inputs(seed: int = 0) -> dict:
    key = jax.random.key(seed)
    ks = jax.random.split(key, 20)
    nrm = lambda k, shape, s: jax.random.normal(k, shape, jnp.float32) * s
    D = D_MODEL
    return {
        "x_prompt": nrm(ks[0], (BATCH, SEQ, D), 1.0),
        "x_sample": nrm(ks[1], (DEC_BATCH, DEC_SEQ, D), 1.0),
        "state_hgrn": nrm(ks[2], (N_A_LAYERS, DEC_BATCH, HG_HEADS, HG_DK, HG_DV), 0.5),
        "cache_k": nrm(ks[3], (N_B_LAYERS, DEC_BATCH, PAST_LEN, SB_HEADS, SB_HEAD_DIM), 1.0),
        "cache_v": nrm(ks[4], (N_B_LAYERS, DEC_BATCH, PAST_LEN, SB_HEADS, SB_HEAD_DIM), 1.0),
        "state_conv": nrm(ks[5], (DEPTH, DEC_BATCH, CONV_WIDTH - 1, D_FF), 1.0),
        "norm_mix": 1.0 + nrm(ks[6], (DEPTH, D), 0.02),
        "norm_ffn": 1.0 + nrm(ks[7], (DEPTH, D), 0.02),
        "norm_final": 1.0 + nrm(ks[8], (D,), 0.02),
        "w_in_a": nrm(ks[9], (N_A_LAYERS, D, 2 * HG_KDIM + 2 * HG_VDIM), D ** -0.5),
        "w_out_a": nrm(ks[10], (N_A_LAYERS, HG_VDIM, D), HG_VDIM ** -0.5),
        "gn_a": 1.0 + nrm(ks[11], (N_A_LAYERS, HG_HEADS, HG_DV), 0.02),
        "lb_logits": nrm(ks[12], (DEPTH + 1, HG_KDIM), 0.1),
        "w_in_b": nrm(ks[13], (N_B_LAYERS, D, 3 * D), D ** -0.5),
        "w_out_b": nrm(ks[14], (N_B_LAYERS, D, D), D ** -0.5),
        "w_up": nrm(ks[15], (DEPTH, D, 2 * D_FF), D ** -0.5),
        "conv_w": nrm(ks[16], (DEPTH, CONV_WIDTH, D_FF), CONV_WIDTH ** -0.5),
        "conv_b": nrm(ks[17], (DEPTH, D_FF), 0.02),
        "w_down": nrm(ks[18], (DEPTH, D_FF, D), D_FF ** -0.5),
    }


def reference(x_prompt, x_sample, state_hgrn, cache_k, cache_v, state_conv,
              norm_mix, norm_ffn, norm_final, w_in_a, w_out_a, gn_a, lb_logits,
              w_in_b, w_out_b, w_up, conv_w, conv_b, w_down):
    dt = x_prompt.dtype
    B = x_prompt.shape[0]
    hg0 = jnp.zeros((N_A_LAYERS, B, HG_HEADS, HG_DK, HG_DV), dt)
    kv0 = jnp.zeros((N_B_LAYERS, B, 0, SB_HEADS, SB_HEAD_DIM), dt)
    conv0 = jnp.zeros((DEPTH, B, CONV_WIDTH - 1, D_FF), dt)
    y_prompt, hg_p, k_p, v_p, conv_p = trunk(
        x_prompt, hg0, kv0, kv0, conv0, norm_mix, norm_ffn, norm_final,
        w_in_a, w_out_a, gn_a, lb_logits, w_in_b, w_out_b, w_up, conv_w, conv_b, w_down)
    y_sample, hg_s, k_s, v_s, conv_s = trunk(
        x_sample, state_hgrn, cache_k, cache_v, state_conv, norm_mix, norm_ffn, norm_final,
        w_in_a, w_out_a, gn_a, lb_logits, w_in_b, w_out_b, w_up, conv_w, conv_b, w_down)
    return (y_prompt, y_sample, hg_p, k_p, v_p, conv_p, hg_s, k_s, v_s, conv_s)
```

```python
import functools

import jax
import jax.numpy as jnp
from jax import lax
from jax.experimental import pallas as pl
from jax.experimental.pallas import tpu as pltpu

F32 = jnp.float32
BF16 = jnp.bfloat16

D_MODEL = 1024
CHUNK = 64
HG_HEADS = 8
HG_DK = 128
SB_HEADS = 16
SB_HEAD_DIM = 64
SB_PAIR = 2 * SB_HEAD_DIM
D_FF = 2816
FF_COLS = 256
CONV_WIDTH = 3
EPS = 1e-6
VMEM_LIMIT = 56 * 1024 * 1024

_NT = (((1,), (1,)), ((), ()))


def _dot(a, b):
    return jnp.dot(a, b, preferred_element_type=F32)


def _dot_nt(a, b):
    return lax.dot_general(a, b, _NT, preferred_element_type=F32)


def _rms(x, g):
    r = lax.rsqrt(jnp.mean(x * x, axis=-1, keepdims=True) + EPS)
    return x * r * g


def _const_spec(shape):
    nd = len(shape)
    return pl.BlockSpec(shape, lambda *_: (0,) * nd, pipeline_mode=pl.Buffered(1))


def _hgrn_kernel(*refs, nb_blk, tt, has_s0, n_lb_rows):
    if has_s0:
        (x_ref, s0_ref, g_ref, win_ref, lbl_ref, gn_ref, wout_ref,
         o_ref, sfin_ref, st_ref, og_ref) = refs
    else:
        (x_ref, g_ref, win_ref, lbl_ref, gn_ref, wout_ref,
         o_ref, sfin_ref, st_ref, og_ref) = refs
        s0_ref = None
    t = pl.program_id(1)
    m = nb_blk * tt

    @pl.when(t == 0)
    def _():
        if has_s0:
            for nb in range(nb_blk):
                for h in range(HG_HEADS):
                    st_ref[nb, h] = s0_ref[nb, h].T
        else:
            st_ref[...] = jnp.zeros_like(st_ref)

    x = x_ref[...].reshape(m, D_MODEL)
    hn = _rms(x, g_ref[...]).astype(BF16)
    proj = _dot(hn, win_ref[...])
    kd = HG_HEADS * HG_DK
    q = proj[:, :kd]
    fpre = proj[:, kd:2 * kd]
    v = proj[:, 2 * kd:3 * kd]
    gate = proj[:, 3 * kd:]

    ll = lbl_ref[...]
    e = jnp.exp(ll - jnp.max(ll, axis=0, keepdims=True))
    sm = e / jnp.sum(e, axis=0, keepdims=True)
    lb = jnp.sum(sm[:n_lb_rows], axis=0, keepdims=True)

    en = jnp.exp(-jnp.abs(fpre))
    r = 1.0 / (1.0 + en)
    er = en * r
    pos = fpre >= 0.0
    sg = jnp.where(pos, r, er)
    sgn = jnp.where(pos, er, r)
    logf = jnp.log(lb + (1.0 - lb) * sg)
    k = (1.0 - lb) * sgn

    ri = lax.broadcasted_iota(jnp.int32, (m, m), 0)
    ci = lax.broadcasted_iota(jnp.int32, (m, m), 1)
    tri = jnp.where(((ri >> 6) == (ci >> 6)) & (ci <= ri), 1.0, 0.0).astype(BF16)
    hi = logf.astype(BF16)
    r1 = logf - hi.astype(F32)
    mid = r1.astype(BF16)
    lo = (r1 - mid.astype(F32)).astype(BF16)
    b = _dot(tri, hi) + _dot(tri, mid) + _dot(tri, lo)

    sgate = jax.nn.sigmoid(gate)
    gn = gn_ref[...]
    t64 = lax.broadcasted_iota(jnp.int32, (CHUNK, CHUNK), 0)
    s64 = lax.broadcasted_iota(jnp.int32, (CHUNK, CHUNK), 1)
    causal = s64 <= t64

    for nb in range(nb_blk):
        for c in range(tt // CHUNK):
            r0 = nb * tt + c * CHUNK
            rows = slice(r0, r0 + CHUNK)
            bc = b[rows]
            bl = bc[CHUNK - 1:CHUNK]
            qt = (q[rows] * jnp.exp(bc)).astype(BF16)
            kt = (k[rows] * jnp.exp(-bc)).astype(BF16)
            ke = (k[rows] * jnp.exp(bl - bc)).astype(BF16)
            ebl = jnp.exp(bl)
            vc = v[rows]
            for h in range(HG_HEADS):
                sl = slice(h * HG_DK, (h + 1) * HG_DK)
                attn = jnp.where(causal, _dot_nt(qt[:, sl], kt[:, sl]), 0.0)
                st = st_ref[nb, h]
                vh = vc[:, sl]
                o = _dot(attn.astype(BF16), vh.astype(BF16)) + _dot_nt(qt[:, sl], st.astype(BF16))
                st_ref[nb, h] = st * ebl[:, sl] + _dot(vh.T.astype(BF16), ke[:, sl])
                on = o * lax.rsqrt(jnp.mean(o * o, axis=-1, keepdims=True) + EPS) * gn[:, sl]
                og_ref[rows, sl] = (on * sgate[rows, sl]).astype(BF16)

    out = x + _dot(og_ref[...], wout_ref[...])
    o_ref[...] = out.reshape(nb_blk, tt, D_MODEL)

    @pl.when(t == pl.num_programs(1) - 1)
    def _():
        for nb in range(nb_blk):
            for h in range(HG_HEADS):
                sfin_ref[nb, h] = st_ref[nb, h].T


def _hgrn_layer(x, s0, norm_g, w_in, lb_logits, gn, w_out, layer_idx, nb_blk, tt):
    B, T, D = x.shape
    has_s0 = s0 is not None
    grid = (B // nb_blk, T // tt)
    x_spec = pl.BlockSpec((nb_blk, tt, D), lambda b, t: (b, t, 0))
    st_spec = pl.BlockSpec((nb_blk, HG_HEADS, HG_DK, HG_DK), lambda b, t: (b, 0, 0, 0))
    in_specs = [x_spec]
    args = [x]
    if has_s0:
        in_specs.append(st_spec)
        args.append(s0)
    in_specs += [_const_spec(norm_g.shape), _const_spec(w_in.shape), _const_spec(lb_logits.shape),
                 _const_spec(gn.shape), _const_spec(w_out.shape)]
    args += [norm_g, w_in, lb_logits, gn, w_out]
    return pl.pallas_call(
        functools.partial(_hgrn_kernel, nb_blk=nb_blk, tt=tt, has_s0=has_s0, n_lb_rows=layer_idx + 1),
        grid=grid,
        in_specs=in_specs,
        out_specs=[x_spec, st_spec],
        out_shape=[jax.ShapeDtypeStruct((B, T, D), F32),
                   jax.ShapeDtypeStruct((B, HG_HEADS, HG_DK, HG_DK), F32)],
        scratch_shapes=[pltpu.VMEM((nb_blk, HG_HEADS, HG_DK, HG_DK), F32),
                        pltpu.VMEM((nb_blk * tt, D), BF16)],
        compiler_params=pltpu.CompilerParams(
            dimension_semantics=("arbitrary", "arbitrary"), vmem_limit_bytes=VMEM_LIMIT),
        name="hgrn_layer",
    )(*args)


def _ffn_kernel(*refs, nb_blk, tt, has_s0, has_pre, final_norm):
    refs = list(refs)
    x_ref = refs.pop(0)
    att_ref = refs.pop(0) if has_pre else None
    wo_ref = refs.pop(0) if has_pre else None
    s0_ref = refs.pop(0) if has_s0 else None
    g_ref, wup_ref, cw_ref, cb_ref, wdn_ref = refs[:5]
    refs = refs[5:]
    gf_ref = refs.pop(0) if final_norm else None
    y_ref, cst_ref, carry_ref, acc_ref = refs
    t = pl.program_id(1)
    m = nb_blk * tt

    @pl.when(t == 0)
    def _():
        if has_s0:
            carry_ref[...] = s0_ref[...]
        else:
            carry_ref[...] = jnp.zeros_like(carry_ref)

    x = x_ref[...].reshape(m, D_MODEL)
    if has_pre:
        x = x + _dot(att_ref[...].reshape(m, D_MODEL), wo_ref[...])
    hn = _rms(x, g_ref[...]).astype(BF16)
    row = lax.broadcasted_iota(jnp.int32, (tt, FF_COLS), 0)

    for j in range(D_FF // FF_COLS):
        cols = slice(j * FF_COLS, (j + 1) * FF_COLS)
        a = _dot(hn, wup_ref[:, cols])
        g = _dot(hn, wup_ref[:, D_FF + j * FF_COLS:D_FF + (j + 1) * FF_COLS])
        w0 = cw_ref[0:1, cols]
        w1 = cw_ref[1:2, cols]
        w2 = cw_ref[2:3, cols]
        bias = cb_ref[:, cols]
        parts = []
        for nb in range(nb_blk):
            an = a[nb * tt:(nb + 1) * tt]
            c0 = carry_ref[nb, 0:1, cols]
            c1 = carry_ref[nb, 1:2, cols]
            am1 = jnp.where(row == 0, c1, pltpu.roll(an, 1, axis=0))
            am2 = jnp.where(row == 0, c0, jnp.where(row == 1, c1, pltpu.roll(an, 2, axis=0)))
            carry_ref[nb, :, cols] = an[tt - 2:tt]
            parts.append(bias + w0 * am2 + w1 * am1 + w2 * an)
        conv = parts[0] if nb_blk == 1 else jnp.concatenate(parts, axis=0)
        hmid = (jax.nn.gelu(conv) * g).astype(BF16)
        upd = _dot(hmid, wdn_ref[cols, :])
        if j == 0:
            acc_ref[...] = upd
        else:
            acc_ref[...] += upd

    out = x + acc_ref[...]
    if final_norm:
        out = _rms(out, gf_ref[...])
    y_ref[...] = out.reshape(nb_blk, tt, D_MODEL)

    @pl.when(t == pl.num_programs(1) - 1)
    def _():
        cst_ref[...] = carry_ref[...]


def _ffn_layer(x, att, w_o, s0, norm_g, w_up, conv_w, conv_b, w_down, norm_final, nb_blk, tt):
    B, T, D = x.shape
    has_pre = att is not None
    has_s0 = s0 is not None
    final_norm = norm_final is not None
    grid = (B // nb_blk, T // tt)
    x_spec = pl.BlockSpec((nb_blk, tt, D), lambda b, t: (b, t, 0))
    cs_spec = pl.BlockSpec((nb_blk, CONV_WIDTH - 1, D_FF), lambda b, t: (b, 0, 0))
    in_specs = [x_spec]
    args = [x]
    if has_pre:
        in_specs += [x_spec, _const_spec(w_o.shape)]
        args += [att, w_o]
    if has_s0:
        in_specs.append(cs_spec)
        args.append(s0)
    in_specs += [_const_spec(norm_g.shape), _const_spec(w_up.shape), _const_spec(conv_w.shape),
                 _const_spec(conv_b.shape), _const_spec(w_down.shape)]
    args += [norm_g, w_up, conv_w, conv_b, w_down]
    if final_norm:
        in_specs.append(_const_spec(norm_final.shape))
        args.append(norm_final)
    return pl.pallas_call(
        functools.partial(_ffn_kernel, nb_blk=nb_blk, tt=tt, has_s0=has_s0, has_pre=has_pre,
                          final_norm=final_norm),
        grid=grid,
        in_specs=in_specs,
        out_specs=[x_spec, cs_spec],
        out_shape=[jax.ShapeDtypeStruct((B, T, D), F32),
                   jax.ShapeDtypeStruct((B, CONV_WIDTH - 1, D_FF), F32)],
        scratch_shapes=[pltpu.VMEM((nb_blk, CONV_WIDTH - 1, D_FF), F32),
                        pltpu.VMEM((nb_blk * tt, D), F32)],
        compiler_params=pltpu.CompilerParams(
            dimension_semantics=("arbitrary", "arbitrary"), vmem_limit_bytes=VMEM_LIMIT),
        name="conv_ffn",
    )(*args)


def _qkv_kernel(x_ref, g_ref, w_ref, q_ref, k_ref, v_ref, kb_ref, vb_ref, *, nb_blk, tt):
    m = nb_blk * tt
    shp = (nb_blk, tt, D_MODEL)
    hn = _rms(x_ref[...].reshape(m, D_MODEL), g_ref[...]).astype(BF16)
    proj = _dot(hn, w_ref[...])
    q_ref[...] = (proj[:, :D_MODEL] * (SB_HEAD_DIM ** -0.5)).astype(BF16).reshape(shp)
    k = proj[:, D_MODEL:2 * D_MODEL]
    v = proj[:, 2 * D_MODEL:]
    k_ref[...] = k.reshape(shp)
    v_ref[...] = v.reshape(shp)
    kb_ref[...] = k.astype(BF16).reshape(shp)
    vb_ref[...] = v.astype(BF16).reshape(shp)


def _qkv_proj(x, norm_g, w_in, nb_blk, tt):
    B, T, D = x.shape
    x_spec = pl.BlockSpec((nb_blk, tt, D), lambda b, t: (b, t, 0))
    return pl.pallas_call(
        functools.partial(_qkv_kernel, nb_blk=nb_blk, tt=tt),
        grid=(B // nb_blk, T // tt),
        in_specs=[x_spec, _const_spec(norm_g.shape), _const_spec(w_in.shape)],
        out_specs=[x_spec] * 5,
        out_shape=[jax.ShapeDtypeStruct((B, T, D), BF16),
                   jax.ShapeDtypeStruct((B, T, D), F32),
                   jax.ShapeDtypeStruct((B, T, D), F32),
                   jax.ShapeDtypeStruct((B, T, D), BF16),
                   jax.ShapeDtypeStruct((B, T, D), BF16)],
        compiler_params=pltpu.CompilerParams(
            dimension_semantics=("arbitrary", "arbitrary"), vmem_limit_bytes=VMEM_LIMIT),
        name="sb_qkv",
    )(x, norm_g, w_in)


def _strict_lower(n):
    j = lax.broadcasted_iota(jnp.int32, (n, n), 0)
    s = lax.broadcasted_iota(jnp.int32, (n, n), 1)
    return jnp.where(j > s, 1.0, 0.0).astype(BF16)


def _split_pair(a):
    lane = lax.broadcasted_iota(jnp.int32, a.shape, 1)
    zero = jnp.zeros_like(a)
    return jnp.where(lane < SB_HEAD_DIM, a, zero), jnp.where(lane >= SB_HEAD_DIM, a, zero)


def _sb_block(q_heads, kb, vb, u, carry, mask):
    acc, run = carry
    v_heads = _split_pair(vb)
    new_run = []
    for qh, vh, rh in zip(q_heads, v_heads, run):
        z = _dot_nt(qh, kb)
        sp = jnp.maximum(z, 0.0) + jnp.log(1.0 + jnp.exp(-jnp.abs(z)))
        p = -sp if mask is None else jnp.where(mask, -sp, 0.0)
        excl = _dot(p.astype(BF16), u)
        a = jnp.exp(z + p + excl + rh)
        if mask is not None:
            a = jnp.where(mask, a, 0.0)
        acc = acc + _dot(a.astype(BF16), vh)
        new_run.append(rh + excl[:, 0:1] + p[:, 0:1])
    return acc, tuple(new_run)


def _attn_prompt_kernel(q_ref, k_ref, v_ref, o_ref, *, tq):
    qi = pl.program_id(2)
    q_heads = _split_pair(q_ref[0])
    u = _strict_lower(tq)
    row = lax.broadcasted_iota(jnp.int32, (tq, tq), 0)
    col = lax.broadcasted_iota(jnp.int32, (tq, tq), 1)
    zero_run = jnp.zeros((tq, 1), F32)
    carry = (jnp.zeros((tq, SB_PAIR), F32), (zero_run, zero_run))

    d0 = pl.multiple_of(qi * tq, tq)
    carry = _sb_block(q_heads, k_ref[0, pl.ds(d0, tq), :], v_ref[0, pl.ds(d0, tq), :], u, carry,
                      col < row)

    def body(i, carry):
        k0 = pl.multiple_of((qi - 1 - i) * tq, tq)
        return _sb_block(q_heads, k_ref[0, pl.ds(k0, tq), :], v_ref[0, pl.ds(k0, tq), :], u, carry,
                         None)

    acc, _ = lax.fori_loop(0, qi, body, carry)
    o_ref[0] = acc.astype(BF16)


def _attn_prompt(q, kb, vb, tq):
    B, T, D = q.shape
    n_pair = D // SB_PAIR
    q_spec = pl.BlockSpec((1, tq, SB_PAIR), lambda b, p, i: (b, i, p))
    kv_spec = pl.BlockSpec((1, T, SB_PAIR), lambda b, p, i: (b, 0, p))
    return pl.pallas_call(
        functools.partial(_attn_prompt_kernel, tq=tq),
        grid=(B, n_pair, T // tq),
        in_specs=[q_spec, kv_spec, kv_spec],
        out_specs=q_spec,
        out_shape=jax.ShapeDtypeStruct((B, T, D), BF16),
        compiler_params=pltpu.CompilerParams(
            dimension_semantics=("arbitrary", "arbitrary", "arbitrary"), vmem_limit_bytes=VMEM_LIMIT),
        name="sb_attn_prompt",
    )(q, kb, vb)


def _attn_sample_kernel(q_ref, kn_ref, vn_ref, kc_ref, vc_ref, o_ref, *, tq, tk, n_past):
    q_heads = _split_pair(q_ref[0])
    row = lax.broadcasted_iota(jnp.int32, (tq, tq), 0)
    col = lax.broadcasted_iota(jnp.int32, (tq, tq), 1)
    zero_run = jnp.zeros((tq, 1), F32)
    carry = (jnp.zeros((tq, SB_PAIR), F32), (zero_run, zero_run))
    carry = _sb_block(q_heads, kn_ref[0], vn_ref[0], _strict_lower(tq), carry, col < row)
    u = _strict_lower(tk)

    def body(i, carry):
        k0 = pl.multiple_of((n_past - 1 - i) * tk, tk)
        kb = kc_ref[0, pl.ds(k0, tk), :].astype(BF16)
        vb = vc_ref[0, pl.ds(k0, tk), :].astype(BF16)
        return _sb_block(q_heads, kb, vb, u, carry, None)

    acc, _ = lax.fori_loop(0, n_past, body, carry)
    o_ref[0] = acc.astype(BF16)


def _attn_sample(q, kb, vb, k_past, v_past, tk):
    B, T, D = q.shape
    P = k_past.shape[1]
    n_pair = D // SB_PAIR
    q_spec = pl.BlockSpec((1, T, SB_PAIR), lambda b, p: (b, 0, p))
    past_spec = pl.BlockSpec((1, P, SB_PAIR), lambda b, p: (b, 0, p))
    return pl.pallas_call(
        functools.partial(_attn_sample_kernel, tq=T, tk=tk, n_past=P // tk),
        grid=(B, n_pair),
        in_specs=[q_spec, q_spec, q_spec, past_spec, past_spec],
        out_specs=q_spec,
        out_shape=jax.ShapeDtypeStruct((B, T, D), BF16),
        compiler_params=pltpu.CompilerParams(
            dimension_semantics=("arbitrary", "arbitrary"), vmem_limit_bytes=VMEM_LIMIT),
        name="sb_attn_sample",
    )(q, kb, vb, k_past, v_past)


def _trunk(x, hg_s0, k_past, v_past, conv_s0, p, nb_blk, tt, tq):
    B, T, D = x.shape
    x1, hg_fin = _hgrn_layer(x, hg_s0, p["norm_mix"][0], p["w_in_a"], p["lb_logits"], p["gn_a"],
                             p["w_out_a"], 0, nb_blk, tt)
    x2, conv0 = _ffn_layer(x1, None, None, None if conv_s0 is None else conv_s0[0],
                           p["norm_ffn"][0], p["w_up"][0], p["conv_w"][0], p["conv_b"][0],
                           p["w_down"][0], None, nb_blk, tt)
    q, k, v, kb, vb = _qkv_proj(x2, p["norm_mix"][1], p["w_in_b"], nb_blk, tt)
    if k_past is None:
        att = _attn_prompt(q, kb, vb, tq)
    else:
        att = _attn_sample(q, kb, vb, k_past, v_past, 256)
    y, conv1 = _ffn_layer(x2, att, p["w_out_b"], None if conv_s0 is None else conv_s0[1],
                          p["norm_ffn"][1], p["w_up"][1], p["conv_w"][1], p["conv_b"][1],
                          p["w_down"][1], p["norm_final"], nb_blk, tt)
    heads = (B, T, SB_HEADS, SB_HEAD_DIM)
    return (y, hg_fin[None], k.reshape(heads)[None], v.reshape(heads)[None],
            jnp.stack([conv0, conv1]))


def kernel(x_prompt, x_sample, state_hgrn, cache_k, cache_v, state_conv, norm_mix, norm_ffn, norm_final,
           w_in_a, w_out_a, gn_a, lb_logits, w_in_b, w_out_b, w_up, conv_w, conv_b, w_down):
    row = lambda a: a.reshape(a.shape[:-1] + (1, a.shape[-1]))
    p = {
        "norm_mix": row(norm_mix), "norm_ffn": row(norm_ffn), "norm_final": row(norm_final),
        "w_in_a": w_in_a[0].astype(BF16), "w_out_a": w_out_a[0].astype(BF16),
        "gn_a": gn_a[0].reshape(1, HG_HEADS * HG_DK), "lb_logits": lb_logits,
        "w_in_b": w_in_b[0].astype(BF16), "w_out_b": w_out_b[0].astype(BF16),
        "w_up": w_up.astype(BF16), "conv_w": conv_w, "conv_b": row(conv_b),
        "w_down": w_down.astype(BF16),
    }
    db, dt, d = x_sample.shape
    past = cache_k.shape[2]
    yp, hgp, kp, vp, cp = _trunk(x_prompt, None, None, None, None, p, 1, 256, 256)
    ys, hgs, ks, vs, cs = _trunk(
        x_sample, state_hgrn[0], cache_k[0].reshape(db, past, d), cache_v[0].reshape(db, past, d),
        state_conv, p, 4, dt, None)
    return (yp, ys, hgp, kp, vp, cp, hgs, ks, vs, cs)
```

```python
import functools

import jax
import jax.numpy as jnp
from jax import lax
from jax.experimental import pallas as pl
from jax.experimental.pallas import tpu as pltpu

F32 = jnp.float32
BF16 = jnp.bfloat16

D_MODEL = 1024
CHUNK = 64
HG_HEADS = 8
HG_DK = 128
SB_HEADS = 16
SB_HEAD_DIM = 64
SB_PAIR = 2 * SB_HEAD_DIM
D_FF = 2816
FF_COLS = 256
CONV_WIDTH = 3
EPS = 1e-6
VMEM_LIMIT = 56 * 1024 * 1024

_NT = (((1,), (1,)), ((), ()))


def _dot(a, b):
    return jnp.dot(a, b, preferred_element_type=F32)


def _dot_nt(a, b):
    return lax.dot_general(a, b, _NT, preferred_element_type=F32)


def _rms(x, g):
    r = lax.rsqrt(jnp.mean(x * x, axis=-1, keepdims=True) + EPS)
    return x * r * g


def _const_spec(shape):
    nd = len(shape)
    return pl.BlockSpec(shape, lambda *_: (0,) * nd, pipeline_mode=pl.Buffered(1))


def _hgrn_kernel(*refs, nb_blk, tt, has_s0, n_lb_rows):
    if has_s0:
        (x_ref, s0_ref, g_ref, win_ref, lbl_ref, gn_ref, wout_ref,
         o_ref, sfin_ref, st_ref, og_ref) = refs
    else:
        (x_ref, g_ref, win_ref, lbl_ref, gn_ref, wout_ref,
         o_ref, sfin_ref, st_ref, og_ref) = refs
        s0_ref = None
    t = pl.program_id(1)
    m = nb_blk * tt

    @pl.when(t == 0)
    def _():
        if has_s0:
            for nb in range(nb_blk):
                for h in range(HG_HEADS):
                    st_ref[nb, h] = s0_ref[nb, h].T
        else:
            st_ref[...] = jnp.zeros_like(st_ref)

    x = x_ref[...].reshape(m, D_MODEL)
    hn = _rms(x, g_ref[...]).astype(BF16)
    proj = _dot(hn, win_ref[...])
    kd = HG_HEADS * HG_DK
    q = proj[:, :kd]
    fpre = proj[:, kd:2 * kd]
    v = proj[:, 2 * kd:3 * kd]
    gate = proj[:, 3 * kd:]

    ll = lbl_ref[...]
    e = jnp.exp(ll - jnp.max(ll, axis=0, keepdims=True))
    sm = e / jnp.sum(e, axis=0, keepdims=True)
    lb = jnp.sum(sm[:n_lb_rows], axis=0, keepdims=True)

    en = jnp.exp(-jnp.abs(fpre))
    r = 1.0 / (1.0 + en)
    er = en * r
    pos = fpre >= 0.0
    sg = jnp.where(pos, r, er)
    sgn = jnp.where(pos, er, r)
    logf = jnp.log(lb + (1.0 - lb) * sg)
    k = (1.0 - lb) * sgn

    ri = lax.broadcasted_iota(jnp.int32, (m, m), 0)
    ci = lax.broadcasted_iota(jnp.int32, (m, m), 1)
    tri = jnp.where(((ri >> 6) == (ci >> 6)) & (ci <= ri), 1.0, 0.0).astype(BF16)
    hi = logf.astype(BF16)
    r1 = logf - hi.astype(F32)
    mid = r1.astype(BF16)
    lo = (r1 - mid.astype(F32)).astype(BF16)
    b = _dot(tri, hi) + _dot(tri, mid) + _dot(tri, lo)

    sgate = jax.nn.sigmoid(gate)
    gn = gn_ref[...]
    t64 = lax.broadcasted_iota(jnp.int32, (CHUNK, CHUNK), 0)
    s64 = lax.broadcasted_iota(jnp.int32, (CHUNK, CHUNK), 1)
    causal = s64 <= t64

    for nb in range(nb_blk):
        for c in range(tt // CHUNK):
            r0 = nb * tt + c * CHUNK
            rows = slice(r0, r0 + CHUNK)
            bc = b[rows]
            bl = bc[CHUNK - 1:CHUNK]
            qt = (q[rows] * jnp.exp(bc)).astype(BF16)
            kt = (k[rows] * jnp.exp(-bc)).astype(BF16)
            ke = (k[rows] * jnp.exp(bl - bc)).astype(BF16)
            ebl = jnp.exp(bl)
            vc = v[rows]
            for h in range(HG_HEADS):
                sl = slice(h * HG_DK, (h + 1) * HG_DK)
                attn = jnp.where(causal, _dot_nt(qt[:, sl], kt[:, sl]), 0.0)
                st = st_ref[nb, h]
                vh = vc[:, sl]
                o = _dot(attn.astype(BF16), vh.astype(BF16)) + _dot_nt(qt[:, sl], st.astype(BF16))
                st_ref[nb, h] = st * ebl[:, sl] + _dot(vh.T.astype(BF16), ke[:, sl])
                on = o * lax.rsqrt(jnp.mean(o * o, axis=-1, keepdims=True) + EPS) * gn[:, sl]
                og_ref[rows, sl] = (on * sgate[rows, sl]).astype(BF16)

    out = x + _dot(og_ref[...], wout_ref[...])
    o_ref[...] = out.reshape(nb_blk, tt, D_MODEL)

    @pl.when(t == pl.num_programs(1) - 1)
    def _():
        for nb in range(nb_blk):
            for h in range(HG_HEADS):
                sfin_ref[nb, h] = st_ref[nb, h].T


def _hgrn_layer(x, s0, norm_g, w_in, lb_logits, gn, w_out, layer_idx, nb_blk, tt):
    B, T, D = x.shape
    has_s0 = s0 is not None
    grid = (B // nb_blk, T // tt)
    x_spec = pl.BlockSpec((nb_blk, tt, D), lambda b, t: (b, t, 0))
    st_spec = pl.BlockSpec((nb_blk, HG_HEADS, HG_DK, HG_DK), lambda b, t: (b, 0, 0, 0))
    in_specs = [x_spec]
    args = [x]
    if has_s0:
        in_specs.append(st_spec)
        args.append(s0)
    in_specs += [_const_spec(norm_g.shape), _const_spec(w_in.shape), _const_spec(lb_logits.shape),
                 _const_spec(gn.shape), _const_spec(w_out.shape)]
    args += [norm_g, w_in, lb_logits, gn, w_out]
    return pl.pallas_call(
        functools.partial(_hgrn_kernel, nb_blk=nb_blk, tt=tt, has_s0=has_s0, n_lb_rows=layer_idx + 1),
        grid=grid,
        in_specs=in_specs,
        out_specs=[x_spec, st_spec],
        out_shape=[jax.ShapeDtypeStruct((B, T, D), F32),
                   jax.ShapeDtypeStruct((B, HG_HEADS, HG_DK, HG_DK), F32)],
        scratch_shapes=[pltpu.VMEM((nb_blk, HG_HEADS, HG_DK, HG_DK), F32),
                        pltpu.VMEM((nb_blk * tt, D), BF16)],
        compiler_params=pltpu.CompilerParams(
            dimension_semantics=("arbitrary", "arbitrary"), vmem_limit_bytes=VMEM_LIMIT),
        name="hgrn_layer",
    )(*args)


def _ffn_kernel(*refs, nb_blk, tt, has_s0, has_pre, final_norm):
    refs = list(refs)
    x_ref = refs.pop(0)
    att_ref = refs.pop(0) if has_pre else None
    wo_ref = refs.pop(0) if has_pre else None
    s0_ref = refs.pop(0) if has_s0 else None
    g_ref, wup_ref, cw_ref, cb_ref, wdn_ref = refs[:5]
    refs = refs[5:]
    gf_ref = refs.pop(0) if final_norm else None
    y_ref, cst_ref, carry_ref, acc_ref = refs
    t = pl.program_id(1)
    m = nb_blk * tt

    @pl.when(t == 0)
    def _():
        if has_s0:
            carry_ref[...] = s0_ref[...]
        else:
            carry_ref[...] = jnp.zeros_like(carry_ref)

    x = x_ref[...].reshape(m, D_MODEL)
    if has_pre:
        x = x + _dot(att_ref[...].reshape(m, D_MODEL), wo_ref[...])
    hn = _rms(x, g_ref[...]).astype(BF16)
    row = lax.broadcasted_iota(jnp.int32, (tt, FF_COLS), 0)

    for j in range(D_FF // FF_COLS):
        cols = slice(j * FF_COLS, (j + 1) * FF_COLS)
        a = _dot(hn, wup_ref[:, cols])
        g = _dot(hn, wup_ref[:, D_FF + j * FF_COLS:D_FF + (j + 1) * FF_COLS])
        w0 = cw_ref[0:1, cols]
        w1 = cw_ref[1:2, cols]
        w2 = cw_ref[2:3, cols]
        bias = cb_ref[:, cols]
        parts = []
        for nb in range(nb_blk):
            an = a[nb * tt:(nb + 1) * tt]
            c0 = carry_ref[nb, 0:1, cols]
            c1 = carry_ref[nb, 1:2, cols]
            am1 = jnp.where(row == 0, c1, pltpu.roll(an, 1, axis=0))
            am2 = jnp.where(row == 0, c0, jnp.where(row == 1, c1, pltpu.roll(an, 2, axis=0)))
            carry_ref[nb, :, cols] = an[tt - 2:tt]
            parts.append(bias + w0 * am2 + w1 * am1 + w2 * an)
        conv = parts[0] if nb_blk == 1 else jnp.concatenate(parts, axis=0)
        hmid = (jax.nn.gelu(conv) * g).astype(BF16)
        upd = _dot(hmid, wdn_ref[cols, :])
        if j == 0:
            acc_ref[...] = upd
        else:
            acc_ref[...] += upd

    out = x + acc_ref[...]
    if final_norm:
        out = _rms(out, gf_ref[...])
    y_ref[...] = out.reshape(nb_blk, tt, D_MODEL)

    @pl.when(t == pl.num_programs(1) - 1)
    def _():
        cst_ref[...] = carry_ref[...]


def _ffn_layer(x, att, w_o, s0, norm_g, w_up, conv_w, conv_b, w_down, norm_final, nb_blk, tt):
    B, T, D = x.shape
    has_pre = att is not None
    has_s0 = s0 is not None
    final_norm = norm_final is not None
    grid = (B // nb_blk, T // tt)
    x_spec = pl.BlockSpec((nb_blk, tt, D), lambda b, t: (b, t, 0))
    cs_spec = pl.BlockSpec((nb_blk, CONV_WIDTH - 1, D_FF), lambda b, t: (b, 0, 0))
    in_specs = [x_spec]
    args = [x]
    if has_pre:
        in_specs += [x_spec, _const_spec(w_o.shape)]
        args += [att, w_o]
    if has_s0:
        in_specs.append(cs_spec)
        args.append(s0)
    in_specs += [_const_spec(norm_g.shape), _const_spec(w_up.shape), _const_spec(conv_w.shape),
                 _const_spec(conv_b.shape), _const_spec(w_down.shape)]
    args += [norm_g, w_up, conv_w, conv_b, w_down]
    if final_norm:
        in_specs.append(_const_spec(norm_final.shape))
        args.append(norm_final)
    return pl.pallas_call(
        functools.partial(_ffn_kernel, nb_blk=nb_blk, tt=tt, has_s0=has_s0, has_pre=has_pre,
                          final_norm=final_norm),
        grid=grid,
        in_specs=in_specs,
        out_specs=[x_spec, cs_spec],
        out_shape=[jax.ShapeDtypeStruct((B, T, D), F32),
                   jax.ShapeDtypeStruct((B, CONV_WIDTH - 1, D_FF), F32)],
        scratch_shapes=[pltpu.VMEM((nb_blk, CONV_WIDTH - 1, D_FF), F32),
                        pltpu.VMEM((nb_blk * tt, D), F32)],
        compiler_params=pltpu.CompilerParams(
            dimension_semantics=("arbitrary", "arbitrary"), vmem_limit_bytes=VMEM_LIMIT),
        name="conv_ffn",
    )(*args)


def _qkv_kernel(x_ref, g_ref, w_ref, q_ref, k_ref, v_ref, kb_ref, vb_ref, *, nb_blk, tt):
    m = nb_blk * tt
    shp = (nb_blk, tt, D_MODEL)
    hn = _rms(x_ref[...].reshape(m, D_MODEL), g_ref[...]).astype(BF16)
    proj = _dot(hn, w_ref[...])
    q_ref[...] = (proj[:, :D_MODEL] * (SB_HEAD_DIM ** -0.5)).astype(BF16).reshape(shp)
    k = proj[:, D_MODEL:2 * D_MODEL]
    v = proj[:, 2 * D_MODEL:]
    k_ref[...] = k.reshape(shp)
    v_ref[...] = v.reshape(shp)
    kb_ref[...] = k.astype(BF16).reshape(shp)
    vb_ref[...] = v.astype(BF16).reshape(shp)


def _qkv_proj(x, norm_g, w_in, nb_blk, tt):
    B, T, D = x.shape
    x_spec = pl.BlockSpec((nb_blk, tt, D), lambda b, t: (b, t, 0))
    return pl.pallas_call(
        functools.partial(_qkv_kernel, nb_blk=nb_blk, tt=tt),
        grid=(B // nb_blk, T // tt),
        in_specs=[x_spec, _const_spec(norm_g.shape), _const_spec(w_in.shape)],
        out_specs=[x_spec] * 5,
        out_shape=[jax.ShapeDtypeStruct((B, T, D), BF16),
                   jax.ShapeDtypeStruct((B, T, D), F32),
                   jax.ShapeDtypeStruct((B, T, D), F32),
                   jax.ShapeDtypeStruct((B, T, D), BF16),
                   jax.ShapeDtypeStruct((B, T, D), BF16)],
        compiler_params=pltpu.CompilerParams(
            dimension_semantics=("arbitrary", "arbitrary"), vmem_limit_bytes=VMEM_LIMIT),
        name="sb_qkv",
    )(x, norm_g, w_in)


def _strict_lower(n):
    j = lax.broadcasted_iota(jnp.int32, (n, n), 0)
    s = lax.broadcasted_iota(jnp.int32, (n, n), 1)
    return jnp.where(j > s, 1.0, 0.0).astype(BF16)


def _split_pair(a):
    lane = lax.broadcasted_iota(jnp.int32, a.shape, 1)
    zero = jnp.zeros_like(a)
    return jnp.where(lane < SB_HEAD_DIM, a, zero), jnp.where(lane >= SB_HEAD_DIM, a, zero)


F32_EXP_ZERO = -104.0


def _any_live(runs):
    m = runs[0]
    for r in runs[1:]:
        m = jnp.maximum(m, r)
    return (jnp.max(m) > F32_EXP_ZERO).astype(jnp.int32)


def _sb_head_block(qh, kh, vh, u, acc, run, mask, kv_t=False):
    z = _dot(qh, kh) if kv_t else _dot_nt(qh, kh)
    sp = jnp.maximum(z, 0.0) + jnp.log(1.0 + jnp.exp(-jnp.abs(z)))
    p = -sp if mask is None else jnp.where(mask, -sp, 0.0)
    excl = _dot(p.astype(BF16), u)
    a = jnp.exp(z + p + excl + run)
    if mask is not None:
        a = jnp.where(mask, a, 0.0)
    av = _dot_nt(a.astype(BF16), vh) if kv_t else _dot(a.astype(BF16), vh)
    return acc + av, run + excl[:, 0:1] + p[:, 0:1]


def _sb_block(q_heads, kb, vb, u, carry, mask):
    acc, run = carry
    new_run = []
    for qh, vh, rh in zip(q_heads, _split_pair(vb), run):
        acc, rh = _sb_head_block(qh, kb, vh, u, acc, rh, mask)
        new_run.append(rh)
    return acc, tuple(new_run)


def _attn_prompt_kernel(q_ref, k_ref, v_ref, o_ref, *, tq):
    qi = pl.program_id(2)
    q_heads = _split_pair(q_ref[0])
    u = _strict_lower(tq)
    row = lax.broadcasted_iota(jnp.int32, (tq, tq), 0)
    col = lax.broadcasted_iota(jnp.int32, (tq, tq), 1)
    zero_run = jnp.zeros((tq, 1), F32)
    carry = (jnp.zeros((tq, SB_PAIR), F32), (zero_run, zero_run))

    d0 = pl.multiple_of(qi * tq, tq)
    carry = _sb_block(q_heads, k_ref[0, pl.ds(d0, tq), :], v_ref[0, pl.ds(d0, tq), :], u, carry,
                      col < row)

    def cond(c):
        j, live, _ = c
        return (j >= 0) & (live > 0)

    def body(c):
        j, _, carry = c
        k0 = pl.multiple_of(j * tq, tq)
        carry = _sb_block(q_heads, k_ref[0, pl.ds(k0, tq), :], v_ref[0, pl.ds(k0, tq), :], u, carry,
                          None)
        return j - 1, _any_live(carry[1]), carry

    _, _, (acc, _) = lax.while_loop(cond, body, (qi - 1, _any_live(carry[1]), carry))
    o_ref[0] = acc.astype(BF16)


def _attn_prompt(q, kb, vb, tq):
    B, T, D = q.shape
    n_pair = D // SB_PAIR
    q_spec = pl.BlockSpec((1, tq, SB_PAIR), lambda b, p, i: (b, i, p))
    kv_spec = pl.BlockSpec((1, T, SB_PAIR), lambda b, p, i: (b, 0, p))
    return pl.pallas_call(
        functools.partial(_attn_prompt_kernel, tq=tq),
        grid=(B, n_pair, T // tq),
        in_specs=[q_spec, kv_spec, kv_spec],
        out_specs=q_spec,
        out_shape=jax.ShapeDtypeStruct((B, T, D), BF16),
        compiler_params=pltpu.CompilerParams(
            dimension_semantics=("arbitrary", "arbitrary", "arbitrary"), vmem_limit_bytes=VMEM_LIMIT),
        name="sb_attn_prompt",
    )(q, kb, vb)


def _attn_sample_kernel(q_ref, kn_ref, vn_ref, kc_hbm, vc_hbm, o_ref, kbuf, vbuf, acc_ref, run_ref, sem,
                        *, tq, tk, n_past):
    b = pl.program_id(0)

    def copies(j, slot):
        k0 = pl.multiple_of(j * tk, tk)
        return (pltpu.make_async_copy(kc_hbm.at[b, :, :, pl.ds(k0, tk)], kbuf.at[slot], sem.at[0, slot]),
                pltpu.make_async_copy(vc_hbm.at[b, :, :, pl.ds(k0, tk)], vbuf.at[slot], sem.at[1, slot]))

    def start(j, slot):
        for c in copies(j, slot):
            c.start()

    def wait(j, slot):
        for c in copies(j, slot):
            c.wait()

    j0 = n_past - 1
    start(j0, 0)

    row = lax.broadcasted_iota(jnp.int32, (tq, tq), 0)
    col = lax.broadcasted_iota(jnp.int32, (tq, tq), 1)
    u_new = _strict_lower(tq)
    q = q_ref[0]
    kn = kn_ref[0]
    vn = vn_ref[0]
    zero_acc = jnp.zeros((tq, SB_HEAD_DIM), F32)
    zero_run = jnp.zeros((tq, 1), F32)
    q_heads = []
    runs = []
    for h in range(SB_HEADS):
        sl = slice(h * SB_HEAD_DIM, (h + 1) * SB_HEAD_DIM)
        q_heads.append(q[:, sl])
        acc, run = _sb_head_block(q_heads[h], kn[:, sl], vn[:, sl], u_new, zero_acc, zero_run, col < row)
        acc_ref[h] = acc
        run_ref[h] = run
        runs.append(run)
    u = _strict_lower(tk)

    def cond(c):
        j, live = c
        return (j >= 0) & (live > 0)

    def body(c):
        j, _ = c
        slot = lax.rem(j0 - j, 2)
        wait(j, slot)

        @pl.when(j > 0)
        def _():
            start(j - 1, 1 - slot)

        runs = []
        for h in range(SB_HEADS):
            kh = kbuf[slot, h].astype(BF16)
            vh = vbuf[slot, h].astype(BF16)
            acc, run = _sb_head_block(q_heads[h], kh, vh, u, acc_ref[h], run_ref[h], None, kv_t=True)
            acc_ref[h] = acc
            run_ref[h] = run
            runs.append(run)
        return j - 1, _any_live(runs)

    jf, _ = lax.while_loop(cond, body, (j0, _any_live(runs)))

    @pl.when(jf >= 0)
    def _():
        wait(jf, lax.rem(j0 - jf, 2))

    o_ref[0] = jnp.concatenate([acc_ref[h] for h in range(SB_HEADS)], axis=1).astype(BF16)


def _attn_sample(q, kb, vb, k_past, v_past, tk):
    B, T, D = q.shape
    n_past = k_past.shape[3] // tk
    q_spec = pl.BlockSpec((1, T, D), lambda b: (b, 0, 0))
    any_spec = pl.BlockSpec(memory_space=pl.ANY)
    return pl.pallas_call(
        functools.partial(_attn_sample_kernel, tq=T, tk=tk, n_past=n_past),
        grid=(B,),
        in_specs=[q_spec, q_spec, q_spec, any_spec, any_spec],
        out_specs=q_spec,
        out_shape=jax.ShapeDtypeStruct((B, T, D), BF16),
        scratch_shapes=[pltpu.VMEM((2, SB_HEADS, SB_HEAD_DIM, tk), F32),
                        pltpu.VMEM((2, SB_HEADS, SB_HEAD_DIM, tk), F32),
                        pltpu.VMEM((SB_HEADS, T, SB_HEAD_DIM), F32),
                        pltpu.VMEM((SB_HEADS, T, 1), F32),
                        pltpu.SemaphoreType.DMA((2, 2))],
        compiler_params=pltpu.CompilerParams(
            dimension_semantics=("arbitrary",), vmem_limit_bytes=VMEM_LIMIT),
        name="sb_attn_sample",
    )(q, kb, vb, k_past, v_past)


def _trunk(x, hg_s0, k_past, v_past, conv_s0, p, nb_blk, tt, tq):
    B, T, D = x.shape
    x1, hg_fin = _hgrn_layer(x, hg_s0, p["norm_mix"][0], p["w_in_a"], p["lb_logits"], p["gn_a"],
                             p["w_out_a"], 0, nb_blk, tt)
    x2, conv0 = _ffn_layer(x1, None, None, None if conv_s0 is None else conv_s0[0],
                           p["norm_ffn"][0], p["w_up"][0], p["conv_w"][0], p["conv_b"][0],
                           p["w_down"][0], None, nb_blk, tt)
    q, k, v, kb, vb = _qkv_proj(x2, p["norm_mix"][1], p["w_in_b"], nb_blk, tt)
    if k_past is None:
        att = _attn_prompt(q, kb, vb, tq)
    else:
        att = _attn_sample(q, kb, vb, k_past, v_past, 256)
    y, conv1 = _ffn_layer(x2, att, p["w_out_b"], None if conv_s0 is None else conv_s0[1],
                          p["norm_ffn"][1], p["w_up"][1], p["conv_w"][1], p["conv_b"][1],
                          p["w_down"][1], p["norm_final"], nb_blk, tt)
    heads = (B, T, SB_HEADS, SB_HEAD_DIM)
    return (y, hg_fin[None], k.reshape(heads)[None], v.reshape(heads)[None],
            jnp.stack([conv0, conv1]))


def kernel(x_prompt, x_sample, state_hgrn, cache_k, cache_v, state_conv, norm_mix, norm_ffn, norm_final,
           w_in_a, w_out_a, gn_a, lb_logits, w_in_b, w_out_b, w_up, conv_w, conv_b, w_down):
    row = lambda a: a.reshape(a.shape[:-1] + (1, a.shape[-1]))
    p = {
        "norm_mix": row(norm_mix), "norm_ffn": row(norm_ffn), "norm_final": row(norm_final),
        "w_in_a": w_in_a[0].astype(BF16), "w_out_a": w_out_a[0].astype(BF16),
        "gn_a": gn_a[0].reshape(1, HG_HEADS * HG_DK), "lb_logits": lb_logits,
        "w_in_b": w_in_b[0].astype(BF16), "w_out_b": w_out_b[0].astype(BF16),
        "w_up": w_up.astype(BF16), "conv_w": conv_w, "conv_b": row(conv_b),
        "w_down": w_down.astype(BF16),
    }
    db, dt, d = x_sample.shape
    past = cache_k.shape[2]
    yp, hgp, kp, vp, cp = _trunk(x_prompt, None, None, None, None, p, 1, 256, 256)
    ys, hgs, ks, vs, cs = _trunk(
        x_sample, state_hgrn[0], jnp.transpose(cache_k[0], (0, 2, 3, 1)),
        jnp.transpose(cache_v[0], (0, 2, 3, 1)), state_conv, p, 4, dt, None)
    return (yp, ys, hgp, kp, vp, cp, hgs, ks, vs, cs)
```

```python
import functools

import jax
import jax.numpy as jnp
from jax import lax
from jax.experimental import pallas as pl
from jax.experimental.pallas import tpu as pltpu

F32 = jnp.float32
BF16 = jnp.bfloat16

D_MODEL = 1024
CHUNK = 64
HG_HEADS = 8
HG_DK = 128
SB_HEADS = 16
SB_HEAD_DIM = 64
SB_PAIR = 2 * SB_HEAD_DIM
D_FF = 2816
FF_COLS = 256
CONV_WIDTH = 3
EPS = 1e-6
VMEM_LIMIT = 56 * 1024 * 1024

_NT = (((1,), (1,)), ((), ()))


def _dot(a, b):
    return jnp.dot(a, b, preferred_element_type=F32)


def _dot_nt(a, b):
    return lax.dot_general(a, b, _NT, preferred_element_type=F32)


def _rms(x, g):
    r = lax.rsqrt(jnp.mean(x * x, axis=-1, keepdims=True) + EPS)
    return x * r * g


def _const_spec(shape):
    nd = len(shape)
    return pl.BlockSpec(shape, lambda *_: (0,) * nd, pipeline_mode=pl.Buffered(1))


def _hgrn_kernel(*refs, nb_blk, tt, has_s0, n_lb_rows):
    if has_s0:
        (x_ref, s0_ref, g_ref, win_ref, lbl_ref, gn_ref, wout_ref,
         o_ref, sfin_ref, st_ref, og_ref) = refs
    else:
        (x_ref, g_ref, win_ref, lbl_ref, gn_ref, wout_ref,
         o_ref, sfin_ref, st_ref, og_ref) = refs
        s0_ref = None
    t = pl.program_id(1)
    m = nb_blk * tt

    @pl.when(t == 0)
    def _():
        if has_s0:
            for nb in range(nb_blk):
                for h in range(HG_HEADS):
                    st_ref[nb, h] = s0_ref[nb, h].T
        else:
            st_ref[...] = jnp.zeros_like(st_ref)

    x = x_ref[...].reshape(m, D_MODEL)
    hn = _rms(x, g_ref[...]).astype(BF16)
    proj = _dot(hn, win_ref[...])
    kd = HG_HEADS * HG_DK
    q = proj[:, :kd]
    fpre = proj[:, kd:2 * kd]
    v = proj[:, 2 * kd:3 * kd]
    gate = proj[:, 3 * kd:]

    ll = lbl_ref[...]
    e = jnp.exp(ll - jnp.max(ll, axis=0, keepdims=True))
    sm = e / jnp.sum(e, axis=0, keepdims=True)
    lb = jnp.sum(sm[:n_lb_rows], axis=0, keepdims=True)

    en = jnp.exp(-jnp.abs(fpre))
    r = 1.0 / (1.0 + en)
    er = en * r
    pos = fpre >= 0.0
    sg = jnp.where(pos, r, er)
    sgn = jnp.where(pos, er, r)
    logf = jnp.log(lb + (1.0 - lb) * sg)
    k = (1.0 - lb) * sgn

    ri = lax.broadcasted_iota(jnp.int32, (m, m), 0)
    ci = lax.broadcasted_iota(jnp.int32, (m, m), 1)
    tri = jnp.where(((ri >> 6) == (ci >> 6)) & (ci <= ri), 1.0, 0.0).astype(BF16)
    hi = logf.astype(BF16)
    r1 = logf - hi.astype(F32)
    mid = r1.astype(BF16)
    lo = (r1 - mid.astype(F32)).astype(BF16)
    b = _dot(tri, hi) + _dot(tri, mid) + _dot(tri, lo)

    sgate = jax.nn.sigmoid(gate)
    gn = gn_ref[...]
    t64 = lax.broadcasted_iota(jnp.int32, (CHUNK, CHUNK), 0)
    s64 = lax.broadcasted_iota(jnp.int32, (CHUNK, CHUNK), 1)
    causal = s64 <= t64

    head_lanes = [slice(h * HG_DK, (h + 1) * HG_DK) for h in range(HG_HEADS)]
    for nb in range(nb_blk):
        sts = [st_ref[nb, h] for h in range(HG_HEADS)]
        for c in range(tt // CHUNK):
            r0 = nb * tt + c * CHUNK
            rows = slice(r0, r0 + CHUNK)
            bc = b[rows]
            bl = bc[CHUNK - 1:CHUNK]
            qt = (q[rows] * jnp.exp(bc)).astype(BF16)
            kt = (k[rows] * jnp.exp(-bc)).astype(BF16)
            ke = (k[rows] * jnp.exp(bl - bc)).astype(BF16)
            ebl = jnp.exp(bl)
            vc = v[rows]
            attns = [jnp.where(causal, _dot_nt(qt[:, sl], kt[:, sl]), 0.0) for sl in head_lanes]
            inters = [_dot_nt(qt[:, sl], st.astype(BF16)) for sl, st in zip(head_lanes, sts)]
            upds = [_dot(vc[:, sl].T.astype(BF16), ke[:, sl]) for sl in head_lanes]
            outs = [_dot(attn.astype(BF16), vc[:, sl].astype(BF16)) + inter
                    for attn, sl, inter in zip(attns, head_lanes, inters)]
            sts = [st * ebl[:, sl] + upd for st, sl, upd in zip(sts, head_lanes, upds)]
            for o, sl in zip(outs, head_lanes):
                on = o * lax.rsqrt(jnp.mean(o * o, axis=-1, keepdims=True) + EPS) * gn[:, sl]
                og_ref[rows, sl] = (on * sgate[rows, sl]).astype(BF16)
        for h in range(HG_HEADS):
            st_ref[nb, h] = sts[h]

    out = x + _dot(og_ref[...], wout_ref[...])
    o_ref[...] = out.reshape(nb_blk, tt, D_MODEL)

    @pl.when(t == pl.num_programs(1) - 1)
    def _():
        for nb in range(nb_blk):
            for h in range(HG_HEADS):
                sfin_ref[nb, h] = st_ref[nb, h].T


def _hgrn_layer(x, s0, norm_g, w_in, lb_logits, gn, w_out, layer_idx, nb_blk, tt):
    B, T, D = x.shape
    has_s0 = s0 is not None
    grid = (B // nb_blk, T // tt)
    x_spec = pl.BlockSpec((nb_blk, tt, D), lambda b, t: (b, t, 0))
    st_spec = pl.BlockSpec((nb_blk, HG_HEADS, HG_DK, HG_DK), lambda b, t: (b, 0, 0, 0))
    in_specs = [x_spec]
    args = [x]
    if has_s0:
        in_specs.append(st_spec)
        args.append(s0)
    in_specs += [_const_spec(norm_g.shape), _const_spec(w_in.shape), _const_spec(lb_logits.shape),
                 _const_spec(gn.shape), _const_spec(w_out.shape)]
    args += [norm_g, w_in, lb_logits, gn, w_out]
    return pl.pallas_call(
        functools.partial(_hgrn_kernel, nb_blk=nb_blk, tt=tt, has_s0=has_s0, n_lb_rows=layer_idx + 1),
        grid=grid,
        in_specs=in_specs,
        out_specs=[x_spec, st_spec],
        out_shape=[jax.ShapeDtypeStruct((B, T, D), F32),
                   jax.ShapeDtypeStruct((B, HG_HEADS, HG_DK, HG_DK), F32)],
        scratch_shapes=[pltpu.VMEM((nb_blk, HG_HEADS, HG_DK, HG_DK), F32),
                        pltpu.VMEM((nb_blk * tt, D), BF16)],
        compiler_params=pltpu.CompilerParams(
            dimension_semantics=("arbitrary", "arbitrary"), vmem_limit_bytes=VMEM_LIMIT),
        name="hgrn_layer",
    )(*args)


def _ffn_kernel(*refs, nb_blk, tt, has_s0, has_pre, final_norm):
    refs = list(refs)
    x_ref = refs.pop(0)
    att_ref = refs.pop(0) if has_pre else None
    wo_ref = refs.pop(0) if has_pre else None
    s0_ref = refs.pop(0) if has_s0 else None
    g_ref, wup_ref, cw_ref, cb_ref, wdn_ref = refs[:5]
    refs = refs[5:]
    gf_ref = refs.pop(0) if final_norm else None
    y_ref, cst_ref, carry_ref, hmid_ref = refs
    t = pl.program_id(1)
    m = nb_blk * tt

    @pl.when(t == 0)
    def _():
        if has_s0:
            carry_ref[...] = s0_ref[...]
        else:
            carry_ref[...] = jnp.zeros_like(carry_ref)

    x = x_ref[...].reshape(m, D_MODEL)
    if has_pre:
        x = x + _dot(att_ref[...].reshape(m, D_MODEL), wo_ref[...])
    hn = _rms(x, g_ref[...]).astype(BF16)
    up = _dot(hn, wup_ref[...])
    row = lax.broadcasted_iota(jnp.int32, (tt, FF_COLS), 0)
    prev = carry_ref[...]

    for j in range(D_FF // FF_COLS):
        cols = slice(j * FF_COLS, (j + 1) * FF_COLS)
        a = up[:, cols]
        g = up[:, D_FF + j * FF_COLS:D_FF + (j + 1) * FF_COLS]
        w0 = cw_ref[0:1, cols]
        w1 = cw_ref[1:2, cols]
        w2 = cw_ref[2:3, cols]
        bias = cb_ref[:, cols]
        for nb in range(nb_blk):
            rows = slice(nb * tt, (nb + 1) * tt)
            an = a[rows]
            c0 = prev[nb, 0:1, cols]
            c1 = prev[nb, 1:2, cols]
            am1 = jnp.where(row == 0, c1, pltpu.roll(an, 1, axis=0))
            am2 = jnp.where(row == 0, c0, jnp.where(row == 1, c1, pltpu.roll(an, 2, axis=0)))
            carry_ref[nb, :, cols] = an[tt - 2:tt]
            conv = bias + w0 * am2 + w1 * am1 + w2 * an
            hmid_ref[rows, cols] = (jax.nn.gelu(conv) * g[rows]).astype(BF16)

    out = x + _dot(hmid_ref[...], wdn_ref[...])
    if final_norm:
        out = _rms(out, gf_ref[...])
    y_ref[...] = out.reshape(nb_blk, tt, D_MODEL)

    @pl.when(t == pl.num_programs(1) - 1)
    def _():
        cst_ref[...] = carry_ref[...]


def _ffn_layer(x, att, w_o, s0, norm_g, w_up, conv_w, conv_b, w_down, norm_final, nb_blk, tt):
    B, T, D = x.shape
    has_pre = att is not None
    has_s0 = s0 is not None
    final_norm = norm_final is not None
    grid = (B // nb_blk, T // tt)
    x_spec = pl.BlockSpec((nb_blk, tt, D), lambda b, t: (b, t, 0))
    cs_spec = pl.BlockSpec((nb_blk, CONV_WIDTH - 1, D_FF), lambda b, t: (b, 0, 0))
    in_specs = [x_spec]
    args = [x]
    if has_pre:
        in_specs += [x_spec, _const_spec(w_o.shape)]
        args += [att, w_o]
    if has_s0:
        in_specs.append(cs_spec)
        args.append(s0)
    in_specs += [_const_spec(norm_g.shape), _const_spec(w_up.shape), _const_spec(conv_w.shape),
                 _const_spec(conv_b.shape), _const_spec(w_down.shape)]
    args += [norm_g, w_up, conv_w, conv_b, w_down]
    if final_norm:
        in_specs.append(_const_spec(norm_final.shape))
        args.append(norm_final)
    return pl.pallas_call(
        functools.partial(_ffn_kernel, nb_blk=nb_blk, tt=tt, has_s0=has_s0, has_pre=has_pre,
                          final_norm=final_norm),
        grid=grid,
        in_specs=in_specs,
        out_specs=[x_spec, cs_spec],
        out_shape=[jax.ShapeDtypeStruct((B, T, D), F32),
                   jax.ShapeDtypeStruct((B, CONV_WIDTH - 1, D_FF), F32)],
        scratch_shapes=[pltpu.VMEM((nb_blk, CONV_WIDTH - 1, D_FF), F32),
                        pltpu.VMEM((nb_blk * tt, D_FF), BF16)],
        compiler_params=pltpu.CompilerParams(
            dimension_semantics=("arbitrary", "arbitrary"), vmem_limit_bytes=VMEM_LIMIT),
        name="conv_ffn",
    )(*args)


def _qkv_kernel(x_ref, g_ref, w_ref, q_ref, k_ref, v_ref, kb_ref, vb_ref, *, nb_blk, tt):
    m = nb_blk * tt
    shp = (nb_blk, tt, D_MODEL)
    hn = _rms(x_ref[...].reshape(m, D_MODEL), g_ref[...]).astype(BF16)
    proj = _dot(hn, w_ref[...])
    q_ref[...] = (proj[:, :D_MODEL] * (SB_HEAD_DIM ** -0.5)).astype(BF16).reshape(shp)
    k = proj[:, D_MODEL:2 * D_MODEL]
    v = proj[:, 2 * D_MODEL:]
    k_ref[...] = k.reshape(shp)
    v_ref[...] = v.reshape(shp)
    kb_ref[...] = k.astype(BF16).reshape(shp)
    vb_ref[...] = v.astype(BF16).reshape(shp)


def _qkv_proj(x, norm_g, w_in, nb_blk, tt):
    B, T, D = x.shape
    x_spec = pl.BlockSpec((nb_blk, tt, D), lambda b, t: (b, t, 0))
    return pl.pallas_call(
        functools.partial(_qkv_kernel, nb_blk=nb_blk, tt=tt),
        grid=(B // nb_blk, T // tt),
        in_specs=[x_spec, _const_spec(norm_g.shape), _const_spec(w_in.shape)],
        out_specs=[x_spec] * 5,
        out_shape=[jax.ShapeDtypeStruct((B, T, D), BF16),
                   jax.ShapeDtypeStruct((B, T, D), F32),
                   jax.ShapeDtypeStruct((B, T, D), F32),
                   jax.ShapeDtypeStruct((B, T, D), BF16),
                   jax.ShapeDtypeStruct((B, T, D), BF16)],
        compiler_params=pltpu.CompilerParams(
            dimension_semantics=("arbitrary", "arbitrary"), vmem_limit_bytes=VMEM_LIMIT),
        name="sb_qkv",
    )(x, norm_g, w_in)


def _strict_lower(n):
    j = lax.broadcasted_iota(jnp.int32, (n, n), 0)
    s = lax.broadcasted_iota(jnp.int32, (n, n), 1)
    return jnp.where(j > s, 1.0, 0.0).astype(BF16)


def _split_pair(a):
    lane = lax.broadcasted_iota(jnp.int32, a.shape, 1)
    zero = jnp.zeros_like(a)
    return jnp.where(lane < SB_HEAD_DIM, a, zero), jnp.where(lane >= SB_HEAD_DIM, a, zero)


F32_EXP_ZERO = 104.0
SIGN_BIT = -2 ** 31


def _any_live(runs):
    m = runs[0]
    for r in runs[1:]:
        m = jnp.minimum(m, r)
    return (jnp.min(m) < F32_EXP_ZERO).astype(jnp.int32)


def _neg_abs(x):
    bits = lax.bitcast_convert_type(x, jnp.int32) | jnp.int32(SIGN_BIT)
    return lax.bitcast_convert_type(bits, F32)


def _sb_heads_block(qs, ks, vs, u, runs, mask, kv_t=False):
    zs = [_dot(q, k) if kv_t else _dot_nt(q, k) for q, k in zip(qs, ks)]
    sps = [jnp.maximum(z, 0.0) + jnp.log(1.0 + jnp.exp(_neg_abs(z))) for z in zs]
    if mask is not None:
        sps = [jnp.where(mask, sp, 0.0) for sp in sps]
    laters = [_dot(sp.astype(BF16), u) for sp in sps]
    ws = [jnp.exp(((z - sp) - later) - run) for z, sp, later, run in zip(zs, sps, laters, runs)]
    if mask is not None:
        ws = [jnp.where(mask, w, 0.0) for w in ws]
    avs = [_dot_nt(w.astype(BF16), v) if kv_t else _dot(w.astype(BF16), v) for w, v in zip(ws, vs)]
    new_runs = [run + later[:, 0:1] + sp[:, 0:1] for run, later, sp in zip(runs, laters, sps)]
    return avs, new_runs


def _attn_prompt_kernel(q_ref, k_ref, v_ref, o_ref, *, tq, n_lock):
    qi = pl.program_id(2)
    lanes = [slice(p * SB_PAIR, (p + 1) * SB_PAIR) for p in range(n_lock)]
    q_all = q_ref[0]
    qs = [qh for sl in lanes for qh in _split_pair(q_all[:, sl])]
    u = _strict_lower(tq)
    row = lax.broadcasted_iota(jnp.int32, (tq, tq), 0)
    col = lax.broadcasted_iota(jnp.int32, (tq, tq), 1)
    accs = tuple(jnp.zeros((tq, SB_PAIR), F32) for _ in lanes)
    runs = tuple(jnp.zeros((tq, 1), F32) for _ in qs)

    def visit(k0, accs, runs, mask):
        kb = k_ref[0, pl.ds(k0, tq), :]
        vb = v_ref[0, pl.ds(k0, tq), :]
        ks = [kb[:, sl] for sl in lanes for _ in range(2)]
        vs = [vh for sl in lanes for vh in _split_pair(vb[:, sl])]
        avs, runs = _sb_heads_block(qs, ks, vs, u, runs, mask)
        accs = tuple(acc + avs[2 * p] + avs[2 * p + 1] for p, acc in enumerate(accs))
        return accs, tuple(runs)

    accs, runs = visit(pl.multiple_of(qi * tq, tq), accs, runs, col < row)

    def cond(c):
        j, alive, _, _ = c
        return (j >= 0) & (alive > 0)

    def body(c):
        j, _, accs, runs = c
        accs, runs = visit(pl.multiple_of(j * tq, tq), accs, runs, None)
        return j - 1, _any_live(runs), accs, runs

    _, _, accs, _ = lax.while_loop(cond, body, (qi - 1, _any_live(runs), accs, runs))
    o_ref[0] = jnp.concatenate(accs, axis=1).astype(BF16)


def _attn_prompt(q, kb, vb, tq, n_lock):
    B, T, D = q.shape
    width = n_lock * SB_PAIR
    q_spec = pl.BlockSpec((1, tq, width), lambda b, p, i: (b, i, p))
    kv_spec = pl.BlockSpec((1, T, width), lambda b, p, i: (b, 0, p))
    return pl.pallas_call(
        functools.partial(_attn_prompt_kernel, tq=tq, n_lock=n_lock),
        grid=(B, D // width, T // tq),
        in_specs=[q_spec, kv_spec, kv_spec],
        out_specs=q_spec,
        out_shape=jax.ShapeDtypeStruct((B, T, D), BF16),
        compiler_params=pltpu.CompilerParams(
            dimension_semantics=("arbitrary", "arbitrary", "arbitrary"), vmem_limit_bytes=VMEM_LIMIT),
        name="sb_attn_prompt",
    )(q, kb, vb)


def _attn_sample_kernel(q_ref, kn_ref, vn_ref, kc_hbm, vc_hbm, o_ref, kbuf, vbuf, acc_ref, run_ref, sem,
                        *, tq, tk, n_past):
    b = pl.program_id(0)

    def copies(j, slot):
        k0 = pl.multiple_of(j * tk, tk)
        return (pltpu.make_async_copy(kc_hbm.at[b, :, :, pl.ds(k0, tk)], kbuf.at[slot], sem.at[0, slot]),
                pltpu.make_async_copy(vc_hbm.at[b, :, :, pl.ds(k0, tk)], vbuf.at[slot], sem.at[1, slot]))

    def start(j, slot):
        for c in copies(j, slot):
            c.start()

    def wait(j, slot):
        for c in copies(j, slot):
            c.wait()

    j0 = n_past - 1
    start(j0, 0)

    row = lax.broadcasted_iota(jnp.int32, (tq, tq), 0)
    col = lax.broadcasted_iota(jnp.int32, (tq, tq), 1)
    u_new = _strict_lower(tq)
    q = q_ref[0]
    kn = kn_ref[0]
    vn = vn_ref[0]
    head_lanes = [slice(h * SB_HEAD_DIM, (h + 1) * SB_HEAD_DIM) for h in range(SB_HEADS)]
    q_heads = [q[:, sl] for sl in head_lanes]
    avs, runs = _sb_heads_block(q_heads, [kn[:, sl] for sl in head_lanes], [vn[:, sl] for sl in head_lanes],
                                u_new, [jnp.zeros((tq, 1), F32)] * SB_HEADS, col < row)
    for h in range(SB_HEADS):
        acc_ref[h] = avs[h]
        run_ref[h] = runs[h]
    u = _strict_lower(tk)

    def cond(c):
        j, live = c
        return (j >= 0) & (live > 0)

    def body(c):
        j, _ = c
        slot = lax.rem(j0 - j, 2)
        wait(j, slot)

        @pl.when(j > 0)
        def _():
            start(j - 1, 1 - slot)

        ks = [kbuf[slot, h].astype(BF16) for h in range(SB_HEADS)]
        vs = [vbuf[slot, h].astype(BF16) for h in range(SB_HEADS)]
        avs, runs = _sb_heads_block(q_heads, ks, vs, u, [run_ref[h] for h in range(SB_HEADS)], None,
                                    kv_t=True)
        for h in range(SB_HEADS):
            acc_ref[h] += avs[h]
            run_ref[h] = runs[h]
        return j - 1, _any_live(runs)

    jf, _ = lax.while_loop(cond, body, (j0, _any_live(runs)))

    @pl.when(jf >= 0)
    def _():
        wait(jf, lax.rem(j0 - jf, 2))

    o_ref[0] = jnp.concatenate([acc_ref[h] for h in range(SB_HEADS)], axis=1).astype(BF16)


def _attn_sample(q, kb, vb, k_past, v_past, tk):
    B, T, D = q.shape
    n_past = k_past.shape[3] // tk
    q_spec = pl.BlockSpec((1, T, D), lambda b: (b, 0, 0))
    any_spec = pl.BlockSpec(memory_space=pl.ANY)
    return pl.pallas_call(
        functools.partial(_attn_sample_kernel, tq=T, tk=tk, n_past=n_past),
        grid=(B,),
        in_specs=[q_spec, q_spec, q_spec, any_spec, any_spec],
        out_specs=q_spec,
        out_shape=jax.ShapeDtypeStruct((B, T, D), BF16),
        scratch_shapes=[pltpu.VMEM((2, SB_HEADS, SB_HEAD_DIM, tk), F32),
                        pltpu.VMEM((2, SB_HEADS, SB_HEAD_DIM, tk), F32),
                        pltpu.VMEM((SB_HEADS, T, SB_HEAD_DIM), F32),
                        pltpu.VMEM((SB_HEADS, T, 1), F32),
                        pltpu.SemaphoreType.DMA((2, 2))],
        compiler_params=pltpu.CompilerParams(
            dimension_semantics=("arbitrary",), vmem_limit_bytes=VMEM_LIMIT),
        name="sb_attn_sample",
    )(q, kb, vb, k_past, v_past)


def _trunk(x, hg_s0, k_past, v_past, conv_s0, p, nb_blk, tt, tq):
    B, T, D = x.shape
    x1, hg_fin = _hgrn_layer(x, hg_s0, p["norm_mix"][0], p["w_in_a"], p["lb_logits"], p["gn_a"],
                             p["w_out_a"], 0, nb_blk, tt)
    x2, conv0 = _ffn_layer(x1, None, None, None if conv_s0 is None else conv_s0[0],
                           p["norm_ffn"][0], p["w_up"][0], p["conv_w"][0], p["conv_b"][0],
                           p["w_down"][0], None, nb_blk, tt)
    q, k, v, kb, vb = _qkv_proj(x2, p["norm_mix"][1], p["w_in_b"], nb_blk, tt)
    if k_past is None:
        att = _attn_prompt(q, kb, vb, tq, 4)
    else:
        att = _attn_sample(q, kb, vb, k_past, v_past, 256)
    y, conv1 = _ffn_layer(x2, att, p["w_out_b"], None if conv_s0 is None else conv_s0[1],
                          p["norm_ffn"][1], p["w_up"][1], p["conv_w"][1], p["conv_b"][1],
                          p["w_down"][1], p["norm_final"], nb_blk, tt)
    heads = (B, T, SB_HEADS, SB_HEAD_DIM)
    return (y, hg_fin[None], k.reshape(heads)[None], v.reshape(heads)[None],
            jnp.stack([conv0, conv1]))


def kernel(x_prompt, x_sample, state_hgrn, cache_k, cache_v, state_conv, norm_mix, norm_ffn, norm_final,
           w_in_a, w_out_a, gn_a, lb_logits, w_in_b, w_out_b, w_up, conv_w, conv_b, w_down):
    row = lambda a: a.reshape(a.shape[:-1] + (1, a.shape[-1]))
    p = {
        "norm_mix": row(norm_mix), "norm_ffn": row(norm_ffn), "norm_final": row(norm_final),
        "w_in_a": w_in_a[0].astype(BF16), "w_out_a": w_out_a[0].astype(BF16),
        "gn_a": gn_a[0].reshape(1, HG_HEADS * HG_DK), "lb_logits": lb_logits,
        "w_in_b": w_in_b[0].astype(BF16), "w_out_b": w_out_b[0].astype(BF16),
        "w_up": w_up.astype(BF16), "conv_w": conv_w, "conv_b": row(conv_b),
        "w_down": w_down.astype(BF16),
    }
    db, dt, d = x_sample.shape
    past = cache_k.shape[2]
    yp, hgp, kp, vp, cp = _trunk(x_prompt, None, None, None, None, p, 1, 256, 256)
    ys, hgs, ks, vs, cs = _trunk(
        x_sample, state_hgrn[0], jnp.transpose(cache_k[0], (0, 2, 3, 1)),
        jnp.transpose(cache_v[0], (0, 2, 3, 1)), state_conv, p, 4, dt, None)
    return (yp, ys, hgp, kp, vp, cp, hgs, ks, vs, cs)
```

```python
import functools

import jax
import jax.numpy as jnp
from jax import lax
from jax.experimental import pallas as pl
from jax.experimental.pallas import tpu as pltpu

F32 = jnp.float32
BF16 = jnp.bfloat16

D_MODEL = 1024
CHUNK = 64
HG_HEADS = 8
HG_DK = 128
HG_UNIT_ROWS = 256
SB_HEADS = 16
SB_HEAD_DIM = 64
SB_PAIR = 2 * SB_HEAD_DIM
D_FF = 2816
FF_COLS = 256
CONV_WIDTH = 3
EPS = 1e-6
VMEM_LIMIT = 56 * 1024 * 1024

_NT = (((1,), (1,)), ((), ()))


def _dot(a, b):
    return jnp.dot(a, b, preferred_element_type=F32)


def _dot_nt(a, b):
    return lax.dot_general(a, b, _NT, preferred_element_type=F32)


def _rms(x, g):
    r = lax.rsqrt(jnp.mean(x * x, axis=-1, keepdims=True) + EPS)
    return x * r * g


def _const_spec(shape):
    nd = len(shape)
    return pl.BlockSpec(shape, lambda *_: (0,) * nd, pipeline_mode=pl.Buffered(1))


def _hgrn_kernel(*refs, nb_blk, tt, has_s0, n_lb_rows):
    if has_s0:
        (x_ref, s0_ref, g_ref, win_ref, lbl_ref, gn_ref, wout_ref,
         o_ref, sfin_ref, st_ref) = refs
    else:
        (x_ref, g_ref, win_ref, lbl_ref, gn_ref, wout_ref,
         o_ref, sfin_ref, st_ref) = refs
        s0_ref = None
    t = pl.program_id(1)
    m = nb_blk * tt
    ur = min(m, HG_UNIT_ROWS)
    kd = HG_HEADS * HG_DK

    @pl.when(t == 0)
    def _():
        if has_s0:
            for nb in range(nb_blk):
                for h in range(HG_HEADS):
                    st_ref[nb, h] = s0_ref[nb, h].T
        else:
            st_ref[...] = jnp.zeros_like(st_ref)

    ll = lbl_ref[...]
    e = jnp.exp(ll - jnp.max(ll, axis=0, keepdims=True))
    sm = e / jnp.sum(e, axis=0, keepdims=True)
    lb = jnp.sum(sm[:n_lb_rows], axis=0, keepdims=True)

    ri = lax.broadcasted_iota(jnp.int32, (ur, ur), 0)
    ci = lax.broadcasted_iota(jnp.int32, (ur, ur), 1)
    tri = jnp.where(((ri >> 6) == (ci >> 6)) & (ci <= ri), 1.0, 0.0).astype(BF16)
    gn = gn_ref[...]
    t64 = lax.broadcasted_iota(jnp.int32, (CHUNK, CHUNK), 0)
    s64 = lax.broadcasted_iota(jnp.int32, (CHUNK, CHUNK), 1)
    causal = s64 <= t64
    head_lanes = [slice(h * HG_DK, (h + 1) * HG_DK) for h in range(HG_HEADS)]
    x_all = x_ref[...].reshape(m, D_MODEL)

    def project(u0):
        x = x_all[u0:u0 + ur]
        hn = _rms(x, g_ref[...]).astype(BF16)
        proj = _dot(hn, win_ref[...])
        q = proj[:, :kd]
        fpre = proj[:, kd:2 * kd]
        v = proj[:, 2 * kd:3 * kd]
        gate = proj[:, 3 * kd:]
        en = jnp.exp(-jnp.abs(fpre))
        r = 1.0 / (1.0 + en)
        er = en * r
        pos = fpre >= 0.0
        logf = jnp.log(lb + (1.0 - lb) * jnp.where(pos, r, er))
        k = (1.0 - lb) * jnp.where(pos, er, r)
        hi = logf.astype(BF16)
        r1 = logf - hi.astype(F32)
        mid = r1.astype(BF16)
        lo = (r1 - mid.astype(F32)).astype(BF16)
        b = _dot(tri, hi) + _dot(tri, mid) + _dot(tri, lo)
        return x, q, k, v, b, jax.nn.sigmoid(gate)

    units = list(range(0, m, ur))
    projected = [project(u0) for u0 in units]
    sts = {}
    for u0, (x, q, k, v, b, sgate) in zip(units, projected):
        ogs = []
        for c in range(ur // CHUNK):
            nb = (u0 + c * CHUNK) // tt
            if nb not in sts:
                sts[nb] = [st_ref[nb, h] for h in range(HG_HEADS)]
            rows = slice(c * CHUNK, (c + 1) * CHUNK)
            bc = b[rows]
            bl = bc[CHUNK - 1:CHUNK]
            qt = (q[rows] * jnp.exp(bc)).astype(BF16)
            kt = (k[rows] * jnp.exp(-bc)).astype(BF16)
            ke = (k[rows] * jnp.exp(bl - bc)).astype(BF16)
            ebl = jnp.exp(bl)
            vc = v[rows]
            attns = [jnp.where(causal, _dot_nt(qt[:, sl], kt[:, sl]), 0.0) for sl in head_lanes]
            inters = [_dot_nt(qt[:, sl], st.astype(BF16)) for sl, st in zip(head_lanes, sts[nb])]
            upds = [_dot(vc[:, sl].T.astype(BF16), ke[:, sl]) for sl in head_lanes]
            outs = [_dot(attn.astype(BF16), vc[:, sl].astype(BF16)) + inter
                    for attn, sl, inter in zip(attns, head_lanes, inters)]
            sts[nb] = [st * ebl[:, sl] + upd for st, sl, upd in zip(sts[nb], head_lanes, upds)]
            ons = [o * lax.rsqrt(jnp.mean(o * o, axis=-1, keepdims=True) + EPS) for o in outs]
            ogs.append((jnp.concatenate(ons, axis=1) * gn * sgate[rows]).astype(BF16))
        out = x + _dot(jnp.concatenate(ogs, axis=0), wout_ref[...])
        if tt >= ur:
            o_ref[u0 // tt, u0 % tt:u0 % tt + ur, :] = out
        else:
            o_ref[u0 // tt:(u0 + ur) // tt] = out.reshape(ur // tt, tt, D_MODEL)
    for nb, st in sts.items():
        for h in range(HG_HEADS):
            st_ref[nb, h] = st[h]

    @pl.when(t == pl.num_programs(1) - 1)
    def _():
        for nb in range(nb_blk):
            for h in range(HG_HEADS):
                sfin_ref[nb, h] = st_ref[nb, h].T


def _hgrn_layer(x, s0, norm_g, w_in, lb_logits, gn, w_out, layer_idx, nb_blk, tt):
    B, T, D = x.shape
    has_s0 = s0 is not None
    grid = (B // nb_blk, T // tt)
    x_spec = pl.BlockSpec((nb_blk, tt, D), lambda b, t: (b, t, 0))
    st_spec = pl.BlockSpec((nb_blk, HG_HEADS, HG_DK, HG_DK), lambda b, t: (b, 0, 0, 0))
    in_specs = [x_spec]
    args = [x]
    if has_s0:
        in_specs.append(st_spec)
        args.append(s0)
    in_specs += [_const_spec(norm_g.shape), _const_spec(w_in.shape), _const_spec(lb_logits.shape),
                 _const_spec(gn.shape), _const_spec(w_out.shape)]
    args += [norm_g, w_in, lb_logits, gn, w_out]
    return pl.pallas_call(
        functools.partial(_hgrn_kernel, nb_blk=nb_blk, tt=tt, has_s0=has_s0, n_lb_rows=layer_idx + 1),
        grid=grid,
        in_specs=in_specs,
        out_specs=[x_spec, st_spec],
        out_shape=[jax.ShapeDtypeStruct((B, T, D), F32),
                   jax.ShapeDtypeStruct((B, HG_HEADS, HG_DK, HG_DK), F32)],
        scratch_shapes=[pltpu.VMEM((nb_blk, HG_HEADS, HG_DK, HG_DK), F32)],
        compiler_params=pltpu.CompilerParams(
            dimension_semantics=("arbitrary", "arbitrary"), vmem_limit_bytes=VMEM_LIMIT),
        name="hgrn_layer",
    )(*args)


def _ffn_kernel(*refs, nb_blk, tt, has_s0, has_pre, final_norm):
    refs = list(refs)
    x_ref = refs.pop(0)
    att_ref = refs.pop(0) if has_pre else None
    wo_ref = refs.pop(0) if has_pre else None
    s0_ref = refs.pop(0) if has_s0 else None
    g_ref, wup_ref, cw_ref, cb_ref, wdn_ref = refs[:5]
    refs = refs[5:]
    gf_ref = refs.pop(0) if final_norm else None
    y_ref, cst_ref, carry_ref, hmid_ref = refs
    t = pl.program_id(1)
    m = nb_blk * tt

    @pl.when(t == 0)
    def _():
        if has_s0:
            carry_ref[...] = s0_ref[...]
        else:
            carry_ref[...] = jnp.zeros_like(carry_ref)

    x = x_ref[...].reshape(m, D_MODEL)
    if has_pre:
        x = x + _dot(att_ref[...].reshape(m, D_MODEL), wo_ref[...])
    hn = _rms(x, g_ref[...]).astype(BF16)
    up = _dot(hn, wup_ref[...])
    row = lax.broadcasted_iota(jnp.int32, (tt, FF_COLS), 0)
    prev = carry_ref[...]

    for j in range(D_FF // FF_COLS):
        cols = slice(j * FF_COLS, (j + 1) * FF_COLS)
        a = up[:, cols]
        g = up[:, D_FF + j * FF_COLS:D_FF + (j + 1) * FF_COLS]
        w0 = cw_ref[0:1, cols]
        w1 = cw_ref[1:2, cols]
        w2 = cw_ref[2:3, cols]
        bias = cb_ref[:, cols]
        for nb in range(nb_blk):
            rows = slice(nb * tt, (nb + 1) * tt)
            an = a[rows]
            c0 = prev[nb, 0:1, cols]
            c1 = prev[nb, 1:2, cols]
            am1 = jnp.where(row == 0, c1, pltpu.roll(an, 1, axis=0))
            am2 = jnp.where(row == 0, c0, jnp.where(row == 1, c1, pltpu.roll(an, 2, axis=0)))
            carry_ref[nb, :, cols] = an[tt - 2:tt]
            conv = bias + w0 * am2 + w1 * am1 + w2 * an
            hmid_ref[rows, cols] = (jax.nn.gelu(conv) * g[rows]).astype(BF16)

    out = x + _dot(hmid_ref[...], wdn_ref[...])
    if final_norm:
        out = _rms(out, gf_ref[...])
    y_ref[...] = out.reshape(nb_blk, tt, D_MODEL)

    @pl.when(t == pl.num_programs(1) - 1)
    def _():
        cst_ref[...] = carry_ref[...]


def _ffn_layer(x, att, w_o, s0, norm_g, w_up, conv_w, conv_b, w_down, norm_final, nb_blk, tt):
    B, T, D = x.shape
    has_pre = att is not None
    has_s0 = s0 is not None
    final_norm = norm_final is not None
    grid = (B // nb_blk, T // tt)
    x_spec = pl.BlockSpec((nb_blk, tt, D), lambda b, t: (b, t, 0))
    cs_spec = pl.BlockSpec((nb_blk, CONV_WIDTH - 1, D_FF), lambda b, t: (b, 0, 0))
    in_specs = [x_spec]
    args = [x]
    if has_pre:
        in_specs += [x_spec, _const_spec(w_o.shape)]
        args += [att, w_o]
    if has_s0:
        in_specs.append(cs_spec)
        args.append(s0)
    in_specs += [_const_spec(norm_g.shape), _const_spec(w_up.shape), _const_spec(conv_w.shape),
                 _const_spec(conv_b.shape), _const_spec(w_down.shape)]
    args += [norm_g, w_up, conv_w, conv_b, w_down]
    if final_norm:
        in_specs.append(_const_spec(norm_final.shape))
        args.append(norm_final)
    return pl.pallas_call(
        functools.partial(_ffn_kernel, nb_blk=nb_blk, tt=tt, has_s0=has_s0, has_pre=has_pre,
                          final_norm=final_norm),
        grid=grid,
        in_specs=in_specs,
        out_specs=[x_spec, cs_spec],
        out_shape=[jax.ShapeDtypeStruct((B, T, D), F32),
                   jax.ShapeDtypeStruct((B, CONV_WIDTH - 1, D_FF), F32)],
        scratch_shapes=[pltpu.VMEM((nb_blk, CONV_WIDTH - 1, D_FF), F32),
                        pltpu.VMEM((nb_blk * tt, D_FF), BF16)],
        compiler_params=pltpu.CompilerParams(
            dimension_semantics=("arbitrary", "arbitrary"), vmem_limit_bytes=VMEM_LIMIT),
        name="conv_ffn",
    )(*args)


def _qkv_kernel(x_ref, g_ref, w_ref, q_ref, kt_ref, vt_ref, kb_ref, vb_ref, *, nb_blk, tt):
    m = nb_blk * tt
    shp = (nb_blk, tt, D_MODEL)
    hn = _rms(x_ref[...].reshape(m, D_MODEL), g_ref[...]).astype(BF16)
    proj = _dot(hn, w_ref[...])
    q_ref[...] = (proj[:, :D_MODEL] * (SB_HEAD_DIM ** -0.5)).astype(BF16).reshape(shp)
    k = proj[:, D_MODEL:2 * D_MODEL]
    v = proj[:, 2 * D_MODEL:]
    for nb in range(nb_blk):
        kt_ref[nb] = k[nb * tt:(nb + 1) * tt].T
        vt_ref[nb] = v[nb * tt:(nb + 1) * tt].T
    kb_ref[...] = k.astype(BF16).reshape(shp)
    vb_ref[...] = v.astype(BF16).reshape(shp)


def _qkv_proj(x, norm_g, w_in, nb_blk, tt):
    B, T, D = x.shape
    x_spec = pl.BlockSpec((nb_blk, tt, D), lambda b, t: (b, t, 0))
    t_spec = pl.BlockSpec((nb_blk, D, tt), lambda b, t: (b, 0, t))
    return pl.pallas_call(
        functools.partial(_qkv_kernel, nb_blk=nb_blk, tt=tt),
        grid=(B // nb_blk, T // tt),
        in_specs=[x_spec, _const_spec(norm_g.shape), _const_spec(w_in.shape)],
        out_specs=[x_spec, t_spec, t_spec, x_spec, x_spec],
        out_shape=[jax.ShapeDtypeStruct((B, T, D), BF16),
                   jax.ShapeDtypeStruct((B, D, T), F32),
                   jax.ShapeDtypeStruct((B, D, T), F32),
                   jax.ShapeDtypeStruct((B, T, D), BF16),
                   jax.ShapeDtypeStruct((B, T, D), BF16)],
        compiler_params=pltpu.CompilerParams(
            dimension_semantics=("arbitrary", "arbitrary"), vmem_limit_bytes=VMEM_LIMIT),
        name="sb_qkv",
    )(x, norm_g, w_in)


def _strict_lower(n):
    j = lax.broadcasted_iota(jnp.int32, (n, n), 0)
    s = lax.broadcasted_iota(jnp.int32, (n, n), 1)
    return jnp.where(j > s, 1.0, 0.0).astype(BF16)


def _split_pair(a):
    lane = lax.broadcasted_iota(jnp.int32, a.shape, 1)
    zero = jnp.zeros_like(a)
    return jnp.where(lane < SB_HEAD_DIM, a, zero), jnp.where(lane >= SB_HEAD_DIM, a, zero)


F32_EXP_ZERO = 104.0


def _any_live(runs):
    m = runs[0]
    for r in runs[1:]:
        m = jnp.minimum(m, r)
    return (jnp.min(m) < F32_EXP_ZERO).astype(jnp.int32)


def _sb_heads_block(qs, ks, vs, u, runs, mask, kv_t=False):
    zs = [_dot(q, k) if kv_t else _dot_nt(q, k) for q, k in zip(qs, ks)]
    sps = [jnp.maximum(z, 0.0) + jnp.log(1.0 + jnp.exp(-jnp.abs(z))) for z in zs]
    if mask is not None:
        sps = [jnp.where(mask, sp, 0.0) for sp in sps]
    laters = [_dot(sp.astype(BF16), u) for sp in sps]
    ws = [jnp.exp(((z - sp) - later) - run) for z, sp, later, run in zip(zs, sps, laters, runs)]
    if mask is not None:
        ws = [jnp.where(mask, w, 0.0) for w in ws]
    avs = [_dot_nt(w.astype(BF16), v) if kv_t else _dot(w.astype(BF16), v) for w, v in zip(ws, vs)]
    new_runs = [run + later[:, 0:1] + sp[:, 0:1] for run, later, sp in zip(runs, laters, sps)]
    return avs, new_runs


def _attn_prompt_kernel(q_ref, k_ref, v_ref, o_ref, *, tq, n_lock):
    qi = pl.program_id(2)
    lanes = [slice(p * SB_PAIR, (p + 1) * SB_PAIR) for p in range(n_lock)]
    q_all = q_ref[0]
    qs = [qh for sl in lanes for qh in _split_pair(q_all[:, sl])]
    u = _strict_lower(tq)
    row = lax.broadcasted_iota(jnp.int32, (tq, tq), 0)
    col = lax.broadcasted_iota(jnp.int32, (tq, tq), 1)
    accs = tuple(jnp.zeros((tq, SB_PAIR), F32) for _ in lanes)
    runs = tuple(jnp.zeros((tq, 1), F32) for _ in qs)

    def visit(k0, accs, runs, mask):
        kb = k_ref[0, pl.ds(k0, tq), :]
        vb = v_ref[0, pl.ds(k0, tq), :]
        ks = [kb[:, sl] for sl in lanes for _ in range(2)]
        vs = [vh for sl in lanes for vh in _split_pair(vb[:, sl])]
        avs, runs = _sb_heads_block(qs, ks, vs, u, runs, mask)
        accs = tuple(acc + avs[2 * p] + avs[2 * p + 1] for p, acc in enumerate(accs))
        return accs, tuple(runs)

    accs, runs = visit(pl.multiple_of(qi * tq, tq), accs, runs, col < row)

    def cond(c):
        j, alive, _, _ = c
        return (j >= 0) & (alive > 0)

    def body(c):
        j, _, accs, runs = c
        accs, runs = visit(pl.multiple_of(j * tq, tq), accs, runs, None)
        return j - 1, _any_live(runs), accs, runs

    _, _, accs, _ = lax.while_loop(cond, body, (qi - 1, _any_live(runs), accs, runs))
    o_ref[0] = jnp.concatenate(accs, axis=1).astype(BF16)


def _attn_prompt(q, kb, vb, tq, n_lock):
    B, T, D = q.shape
    width = n_lock * SB_PAIR
    q_spec = pl.BlockSpec((1, tq, width), lambda b, p, i: (b, i, p))
    kv_spec = pl.BlockSpec((1, T, width), lambda b, p, i: (b, 0, p))
    return pl.pallas_call(
        functools.partial(_attn_prompt_kernel, tq=tq, n_lock=n_lock),
        grid=(B, D // width, T // tq),
        in_specs=[q_spec, kv_spec, kv_spec],
        out_specs=q_spec,
        out_shape=jax.ShapeDtypeStruct((B, T, D), BF16),
        compiler_params=pltpu.CompilerParams(
            dimension_semantics=("arbitrary", "arbitrary", "arbitrary"), vmem_limit_bytes=VMEM_LIMIT),
        name="sb_attn_prompt",
    )(q, kb, vb)


def _attn_sample_kernel(q_ref, kn_ref, vn_ref, kc_hbm, vc_hbm, o_ref, kbuf, vbuf, acc_ref, run_ref, sem,
                        *, tq, tk, n_past):
    b = pl.program_id(0)

    def copies(j, slot):
        k0 = pl.multiple_of(j * tk, tk)
        return (pltpu.make_async_copy(kc_hbm.at[b, :, :, pl.ds(k0, tk)], kbuf.at[slot], sem.at[0, slot]),
                pltpu.make_async_copy(vc_hbm.at[b, :, :, pl.ds(k0, tk)], vbuf.at[slot], sem.at[1, slot]))

    def start(j, slot):
        for c in copies(j, slot):
            c.start()

    def wait(j, slot):
        for c in copies(j, slot):
            c.wait()

    j0 = n_past - 1
    start(j0, 0)

    row = lax.broadcasted_iota(jnp.int32, (tq, tq), 0)
    col = lax.broadcasted_iota(jnp.int32, (tq, tq), 1)
    u_new = _strict_lower(tq)
    q = q_ref[0]
    kn = kn_ref[0]
    vn = vn_ref[0]
    head_lanes = [slice(h * SB_HEAD_DIM, (h + 1) * SB_HEAD_DIM) for h in range(SB_HEADS)]
    q_heads = [q[:, sl] for sl in head_lanes]
    avs, runs = _sb_heads_block(q_heads, [kn[:, sl] for sl in head_lanes], [vn[:, sl] for sl in head_lanes],
                                u_new, [jnp.zeros((tq, 1), F32)] * SB_HEADS, col < row)
    for h in range(SB_HEADS):
        acc_ref[h] = avs[h]
        run_ref[h] = runs[h]
    u = _strict_lower(tk)

    def cond(c):
        j, live = c
        return (j >= 0) & (live > 0)

    def body(c):
        j, _ = c
        slot = lax.rem(j0 - j, 2)
        wait(j, slot)

        @pl.when(j > 0)
        def _():
            start(j - 1, 1 - slot)

        ks = [kbuf[slot, h].astype(BF16) for h in range(SB_HEADS)]
        vs = [vbuf[slot, h].astype(BF16) for h in range(SB_HEADS)]
        avs, runs = _sb_heads_block(q_heads, ks, vs, u, [run_ref[h] for h in range(SB_HEADS)], None,
                                    kv_t=True)
        for h in range(SB_HEADS):
            acc_ref[h] += avs[h]
            run_ref[h] = runs[h]
        return j - 1, _any_live(runs)

    jf, _ = lax.while_loop(cond, body, (j0, _any_live(runs)))

    @pl.when(jf >= 0)
    def _():
        wait(jf, lax.rem(j0 - jf, 2))

    o_ref[0] = jnp.concatenate([acc_ref[h] for h in range(SB_HEADS)], axis=1).astype(BF16)


def _attn_sample(q, kb, vb, k_past, v_past, tk):
    B, T, D = q.shape
    n_past = k_past.shape[3] // tk
    q_spec = pl.BlockSpec((1, T, D), lambda b: (b, 0, 0))
    any_spec = pl.BlockSpec(memory_space=pl.ANY)
    return pl.pallas_call(
        functools.partial(_attn_sample_kernel, tq=T, tk=tk, n_past=n_past),
        grid=(B,),
        in_specs=[q_spec, q_spec, q_spec, any_spec, any_spec],
        out_specs=q_spec,
        out_shape=jax.ShapeDtypeStruct((B, T, D), BF16),
        scratch_shapes=[pltpu.VMEM((2, SB_HEADS, SB_HEAD_DIM, tk), F32),
                        pltpu.VMEM((2, SB_HEADS, SB_HEAD_DIM, tk), F32),
                        pltpu.VMEM((SB_HEADS, T, SB_HEAD_DIM), F32),
                        pltpu.VMEM((SB_HEADS, T, 1), F32),
                        pltpu.SemaphoreType.DMA((2, 2))],
        compiler_params=pltpu.CompilerParams(
            dimension_semantics=("arbitrary",), vmem_limit_bytes=VMEM_LIMIT),
        name="sb_attn_sample",
    )(q, kb, vb, k_past, v_past)


def _trunk(x, hg_s0, k_past, v_past, conv_s0, p, nb_blk, tt, tq, hg_tiles=1):
    B, T, D = x.shape
    x1, hg_fin = _hgrn_layer(x, hg_s0, p["norm_mix"][0], p["w_in_a"], p["lb_logits"], p["gn_a"],
                             p["w_out_a"], 0, nb_blk, tt * hg_tiles)
    x2, conv0 = _ffn_layer(x1, None, None, None if conv_s0 is None else conv_s0[0],
                           p["norm_ffn"][0], p["w_up"][0], p["conv_w"][0], p["conv_b"][0],
                           p["w_down"][0], None, nb_blk, tt)
    q, kt, vt, kb, vb = _qkv_proj(x2, p["norm_mix"][1], p["w_in_b"], nb_blk, tt)
    if k_past is None:
        att = _attn_prompt(q, kb, vb, tq, 4)
    else:
        att = _attn_sample(q, kb, vb, k_past, v_past, 256)
    y, conv1 = _ffn_layer(x2, att, p["w_out_b"], None if conv_s0 is None else conv_s0[1],
                          p["norm_ffn"][1], p["w_up"][1], p["conv_w"][1], p["conv_b"][1],
                          p["w_down"][1], p["norm_final"], nb_blk, tt)
    to_rows = lambda a: jnp.transpose(a.reshape(B, SB_HEADS, SB_HEAD_DIM, T), (0, 3, 1, 2))[None]
    return (y, hg_fin[None], to_rows(kt), to_rows(vt), jnp.stack([conv0, conv1]))


def kernel(x_prompt, x_sample, state_hgrn, cache_k, cache_v, state_conv, norm_mix, norm_ffn, norm_final,
           w_in_a, w_out_a, gn_a, lb_logits, w_in_b, w_out_b, w_up, conv_w, conv_b, w_down):
    row = lambda a: a.reshape(a.shape[:-1] + (1, a.shape[-1]))
    p = {
        "norm_mix": row(norm_mix), "norm_ffn": row(norm_ffn), "norm_final": row(norm_final),
        "w_in_a": w_in_a[0].astype(BF16), "w_out_a": w_out_a[0].astype(BF16),
        "gn_a": gn_a[0].reshape(1, HG_HEADS * HG_DK), "lb_logits": lb_logits,
        "w_in_b": w_in_b[0].astype(BF16), "w_out_b": w_out_b[0].astype(BF16),
        "w_up": w_up.astype(BF16), "conv_w": conv_w, "conv_b": row(conv_b),
        "w_down": w_down.astype(BF16),
    }
    dt = x_sample.shape[1]
    yp, hgp, kp, vp, cp = _trunk(x_prompt, None, None, None, None, p, 1, 256, 256, hg_tiles=2)
    ys, hgs, ks, vs, cs = _trunk(
        x_sample, state_hgrn[0], jnp.transpose(cache_k[0], (0, 2, 3, 1)),
        jnp.transpose(cache_v[0], (0, 2, 3, 1)), state_conv, p, 4, dt, None)
    return (yp, ys, hgp, kp, vp, cp, hgs, ks, vs, cs)
```

```python
import functools

import jax
import jax.numpy as jnp
from jax import lax
from jax.experimental import pallas as pl
from jax.experimental.pallas import tpu as pltpu

F32 = jnp.float32
BF16 = jnp.bfloat16

D_MODEL = 1024
CHUNK = 64
HG_HEADS = 8
HG_DK = 128
HG_UNIT_ROWS = 256
HG_STREAMS_PER_STEP = 4
SB_HEADS = 16
SB_HEAD_DIM = 64
SB_PAIR = 2 * SB_HEAD_DIM
D_FF = 2816
FF_COLS = 256
CONV_WIDTH = 3
EPS = 1e-6
VMEM_LIMIT = 56 * 1024 * 1024
ROWS_PER_STEP = 512
SB_TILE = 256
SB_LOCKSTEP_PAIRS = 4

_NT = (((1,), (1,)), ((), ()))


def _dot(a, b):
    return jnp.dot(a, b, preferred_element_type=F32)


def _dot_nt(a, b):
    return lax.dot_general(a, b, _NT, preferred_element_type=F32)


def _rms(x, g):
    r = lax.rsqrt(jnp.mean(x * x, axis=-1, keepdims=True) + EPS)
    return x * r * g


def _const_spec(shape):
    nd = len(shape)
    return pl.BlockSpec(shape, lambda *_: (0,) * nd, pipeline_mode=pl.Buffered(1))


def _hgrn_kernel(*refs, nb_blk, tt, has_s0, n_lb_rows):
    if has_s0:
        (x_ref, s0_ref, g_ref, win_ref, lbl_ref, gn_ref, wout_ref,
         o_ref, sfin_ref, st_ref) = refs
    else:
        (x_ref, g_ref, win_ref, lbl_ref, gn_ref, wout_ref,
         o_ref, sfin_ref, st_ref) = refs
        s0_ref = None
    t = pl.program_id(1)
    m = nb_blk * tt
    ur = min(m, HG_UNIT_ROWS)
    kd = HG_HEADS * HG_DK

    @pl.when(t == 0)
    def _():
        if has_s0:
            for nb in range(nb_blk):
                for h in range(HG_HEADS):
                    st_ref[nb, h] = s0_ref[nb, h].T
        else:
            st_ref[...] = jnp.zeros_like(st_ref)

    ll = lbl_ref[...]
    e = jnp.exp(ll - jnp.max(ll, axis=0, keepdims=True))
    sm = e / jnp.sum(e, axis=0, keepdims=True)
    lb = jnp.sum(sm[:n_lb_rows], axis=0, keepdims=True)

    ri = lax.broadcasted_iota(jnp.int32, (ur, ur), 0)
    ci = lax.broadcasted_iota(jnp.int32, (ur, ur), 1)
    tri = jnp.where(((ri >> 6) == (ci >> 6)) & (ci <= ri), 1.0, 0.0).astype(BF16)
    gn = gn_ref[...]
    t64 = lax.broadcasted_iota(jnp.int32, (CHUNK, CHUNK), 0)
    s64 = lax.broadcasted_iota(jnp.int32, (CHUNK, CHUNK), 1)
    causal = s64 <= t64
    head_lanes = [slice(h * HG_DK, (h + 1) * HG_DK) for h in range(HG_HEADS)]
    x_all = x_ref[...].reshape(m, D_MODEL)

    def project(u0):
        x = x_all[u0:u0 + ur]
        hn = _rms(x, g_ref[...]).astype(BF16)
        proj = _dot(hn, win_ref[...])
        q = proj[:, :kd]
        fpre = proj[:, kd:2 * kd]
        v = proj[:, 2 * kd:3 * kd]
        gate = proj[:, 3 * kd:]
        en = jnp.exp(-jnp.abs(fpre))
        r = 1.0 / (1.0 + en)
        er = en * r
        pos = fpre >= 0.0
        logf = jnp.log(lb + (1.0 - lb) * jnp.where(pos, r, er))
        k = (1.0 - lb) * jnp.where(pos, er, r)
        hi = logf.astype(BF16)
        r1 = logf - hi.astype(F32)
        mid = r1.astype(BF16)
        lo = (r1 - mid.astype(F32)).astype(BF16)
        b = _dot(tri, hi) + _dot(tri, mid) + _dot(tri, lo)
        return x, q, k, v, b, jax.nn.sigmoid(gate)

    units = list(range(0, m, ur))
    projected = [project(u0) for u0 in units]
    sts = {}
    for u0, (x, q, k, v, b, sgate) in zip(units, projected):
        ogs = []
        for c in range(ur // CHUNK):
            nb = (u0 + c * CHUNK) // tt
            if nb not in sts:
                sts[nb] = [st_ref[nb, h] for h in range(HG_HEADS)]
            rows = slice(c * CHUNK, (c + 1) * CHUNK)
            bc = b[rows]
            bl = bc[CHUNK - 1:CHUNK]
            qt = (q[rows] * jnp.exp(bc)).astype(BF16)
            kt = (k[rows] * jnp.exp(-bc)).astype(BF16)
            ke = (k[rows] * jnp.exp(bl - bc)).astype(BF16)
            ebl = jnp.exp(bl)
            vc = v[rows]
            attns = [jnp.where(causal, _dot_nt(qt[:, sl], kt[:, sl]), 0.0) for sl in head_lanes]
            inters = [_dot_nt(qt[:, sl], st.astype(BF16)) for sl, st in zip(head_lanes, sts[nb])]
            upds = [_dot(vc[:, sl].T.astype(BF16), ke[:, sl]) for sl in head_lanes]
            outs = [_dot(attn.astype(BF16), vc[:, sl].astype(BF16)) + inter
                    for attn, sl, inter in zip(attns, head_lanes, inters)]
            sts[nb] = [st * ebl[:, sl] + upd for st, sl, upd in zip(sts[nb], head_lanes, upds)]
            ons = [o * lax.rsqrt(jnp.mean(o * o, axis=-1, keepdims=True) + EPS) for o in outs]
            ogs.append((jnp.concatenate(ons, axis=1) * gn * sgate[rows]).astype(BF16))
        out = x + _dot(jnp.concatenate(ogs, axis=0), wout_ref[...])
        if tt >= ur:
            o_ref[u0 // tt, u0 % tt:u0 % tt + ur, :] = out
        else:
            o_ref[u0 // tt:(u0 + ur) // tt] = out.reshape(ur // tt, tt, D_MODEL)
    for nb, st in sts.items():
        for h in range(HG_HEADS):
            st_ref[nb, h] = st[h]

    @pl.when(t == pl.num_programs(1) - 1)
    def _():
        for nb in range(nb_blk):
            for h in range(HG_HEADS):
                sfin_ref[nb, h] = st_ref[nb, h].T


def _hgrn_layer(x, s0, norm_g, w_in, lb_logits, gn, w_out, layer_idx, nb_blk, tt):
    B, T, D = x.shape
    has_s0 = s0 is not None
    grid = (B // nb_blk, T // tt)
    x_spec = pl.BlockSpec((nb_blk, tt, D), lambda b, t: (b, t, 0))
    st_spec = pl.BlockSpec((nb_blk, HG_HEADS, HG_DK, HG_DK), lambda b, t: (b, 0, 0, 0))
    in_specs = [x_spec]
    args = [x]
    if has_s0:
        in_specs.append(st_spec)
        args.append(s0)
    in_specs += [_const_spec(norm_g.shape), _const_spec(w_in.shape), _const_spec(lb_logits.shape),
                 _const_spec(gn.shape), _const_spec(w_out.shape)]
    args += [norm_g, w_in, lb_logits, gn, w_out]
    return pl.pallas_call(
        functools.partial(_hgrn_kernel, nb_blk=nb_blk, tt=tt, has_s0=has_s0, n_lb_rows=layer_idx + 1),
        grid=grid,
        in_specs=in_specs,
        out_specs=[x_spec, st_spec],
        out_shape=[jax.ShapeDtypeStruct((B, T, D), F32),
                   jax.ShapeDtypeStruct((B, HG_HEADS, HG_DK, HG_DK), F32)],
        scratch_shapes=[pltpu.VMEM((nb_blk, HG_HEADS, HG_DK, HG_DK), F32)],
        compiler_params=pltpu.CompilerParams(
            dimension_semantics=("arbitrary", "arbitrary"), vmem_limit_bytes=VMEM_LIMIT),
        name="hgrn_layer",
    )(*args)


def _ffn_kernel(*refs, nb_blk, tt, has_s0, has_pre, final_norm):
    refs = list(refs)
    x_ref = refs.pop(0)
    att_ref = refs.pop(0) if has_pre else None
    wo_ref = refs.pop(0) if has_pre else None
    s0_ref = refs.pop(0) if has_s0 else None
    g_ref, wup_ref, cw_ref, cb_ref, wdn_ref = refs[:5]
    refs = refs[5:]
    gf_ref = refs.pop(0) if final_norm else None
    y_ref, cst_ref, carry_ref, hmid_ref = refs
    t = pl.program_id(1)
    m = nb_blk * tt

    @pl.when(t == 0)
    def _():
        if has_s0:
            carry_ref[...] = s0_ref[...]
        else:
            carry_ref[...] = jnp.zeros_like(carry_ref)

    x = x_ref[...].reshape(m, D_MODEL)
    if has_pre:
        x = x + _dot(att_ref[...].reshape(m, D_MODEL), wo_ref[...])
    hn = _rms(x, g_ref[...]).astype(BF16)
    up = _dot(hn, wup_ref[...])
    row = lax.broadcasted_iota(jnp.int32, (tt, FF_COLS), 0)
    prev = carry_ref[...]

    for j in range(D_FF // FF_COLS):
        cols = slice(j * FF_COLS, (j + 1) * FF_COLS)
        a = up[:, cols]
        g = up[:, D_FF + j * FF_COLS:D_FF + (j + 1) * FF_COLS]
        w0 = cw_ref[0:1, cols]
        w1 = cw_ref[1:2, cols]
        w2 = cw_ref[2:3, cols]
        bias = cb_ref[:, cols]
        for nb in range(nb_blk):
            rows = slice(nb * tt, (nb + 1) * tt)
            an = a[rows]
            c0 = prev[nb, 0:1, cols]
            c1 = prev[nb, 1:2, cols]
            am1 = jnp.where(row == 0, c1, pltpu.roll(an, 1, axis=0))
            am2 = jnp.where(row == 0, c0, jnp.where(row == 1, c1, pltpu.roll(an, 2, axis=0)))
            carry_ref[nb, :, cols] = an[tt - 2:tt]
            conv = bias + w0 * am2 + w1 * am1 + w2 * an
            hmid_ref[rows, cols] = (jax.nn.gelu(conv) * g[rows]).astype(BF16)

    out = x + _dot(hmid_ref[...], wdn_ref[...])
    if final_norm:
        out = _rms(out, gf_ref[...])
    y_ref[...] = out.reshape(nb_blk, tt, D_MODEL)

    @pl.when(t == pl.num_programs(1) - 1)
    def _():
        cst_ref[...] = carry_ref[...]


def _ffn_layer(x, att, w_o, s0, norm_g, w_up, conv_w, conv_b, w_down, norm_final, nb_blk, tt):
    B, T, D = x.shape
    has_pre = att is not None
    has_s0 = s0 is not None
    final_norm = norm_final is not None
    grid = (B // nb_blk, T // tt)
    x_spec = pl.BlockSpec((nb_blk, tt, D), lambda b, t: (b, t, 0))
    cs_spec = pl.BlockSpec((nb_blk, CONV_WIDTH - 1, D_FF), lambda b, t: (b, 0, 0))
    in_specs = [x_spec]
    args = [x]
    if has_pre:
        in_specs += [x_spec, _const_spec(w_o.shape)]
        args += [att, w_o]
    if has_s0:
        in_specs.append(cs_spec)
        args.append(s0)
    in_specs += [_const_spec(norm_g.shape), _const_spec(w_up.shape), _const_spec(conv_w.shape),
                 _const_spec(conv_b.shape), _const_spec(w_down.shape)]
    args += [norm_g, w_up, conv_w, conv_b, w_down]
    if final_norm:
        in_specs.append(_const_spec(norm_final.shape))
        args.append(norm_final)
    return pl.pallas_call(
        functools.partial(_ffn_kernel, nb_blk=nb_blk, tt=tt, has_s0=has_s0, has_pre=has_pre,
                          final_norm=final_norm),
        grid=grid,
        in_specs=in_specs,
        out_specs=[x_spec, cs_spec],
        out_shape=[jax.ShapeDtypeStruct((B, T, D), F32),
                   jax.ShapeDtypeStruct((B, CONV_WIDTH - 1, D_FF), F32)],
        scratch_shapes=[pltpu.VMEM((nb_blk, CONV_WIDTH - 1, D_FF), F32),
                        pltpu.VMEM((nb_blk * tt, D_FF), BF16)],
        compiler_params=pltpu.CompilerParams(
            dimension_semantics=("arbitrary", "arbitrary"), vmem_limit_bytes=VMEM_LIMIT),
        name="conv_ffn",
    )(*args)


def _qkv_kernel(x_ref, g_ref, w_ref, q_ref, k_ref, v_ref, kb_ref, vb_ref, *, nb_blk, tt, feature_major):
    m = nb_blk * tt
    shp = (nb_blk, tt, D_MODEL)
    hn = _rms(x_ref[...].reshape(m, D_MODEL), g_ref[...]).astype(BF16)
    proj = _dot(hn, w_ref[...])
    q_ref[...] = (proj[:, :D_MODEL] * (SB_HEAD_DIM ** -0.5)).astype(BF16).reshape(shp)
    k = proj[:, D_MODEL:2 * D_MODEL]
    v = proj[:, 2 * D_MODEL:]
    if feature_major:
        for nb in range(nb_blk):
            k_ref[nb] = k[nb * tt:(nb + 1) * tt].T
            v_ref[nb] = v[nb * tt:(nb + 1) * tt].T
    else:
        k_ref[...] = k.reshape(shp)
        v_ref[...] = v.reshape(shp)
    kb_ref[...] = k.astype(BF16).reshape(shp)
    vb_ref[...] = v.astype(BF16).reshape(shp)


def _qkv_proj(x, norm_g, w_in, nb_blk, tt, feature_major):
    B, T, D = x.shape
    x_spec = pl.BlockSpec((nb_blk, tt, D), lambda b, t: (b, t, 0))
    kv_spec = pl.BlockSpec((nb_blk, D, tt), lambda b, t: (b, 0, t)) if feature_major else x_spec
    kv_shape = (B, D, T) if feature_major else (B, T, D)
    return pl.pallas_call(
        functools.partial(_qkv_kernel, nb_blk=nb_blk, tt=tt, feature_major=feature_major),
        grid=(B // nb_blk, T // tt),
        in_specs=[x_spec, _const_spec(norm_g.shape), _const_spec(w_in.shape)],
        out_specs=[x_spec, kv_spec, kv_spec, x_spec, x_spec],
        out_shape=[jax.ShapeDtypeStruct((B, T, D), BF16),
                   jax.ShapeDtypeStruct(kv_shape, F32),
                   jax.ShapeDtypeStruct(kv_shape, F32),
                   jax.ShapeDtypeStruct((B, T, D), BF16),
                   jax.ShapeDtypeStruct((B, T, D), BF16)],
        compiler_params=pltpu.CompilerParams(
            dimension_semantics=("arbitrary", "arbitrary"), vmem_limit_bytes=VMEM_LIMIT),
        name="sb_qkv",
    )(x, norm_g, w_in)


def _strict_lower(n):
    j = lax.broadcasted_iota(jnp.int32, (n, n), 0)
    s = lax.broadcasted_iota(jnp.int32, (n, n), 1)
    return jnp.where(j > s, 1.0, 0.0).astype(BF16)


def _split_pair(a):
    lane = lax.broadcasted_iota(jnp.int32, a.shape, 1)
    zero = jnp.zeros_like(a)
    return jnp.where(lane < SB_HEAD_DIM, a, zero), jnp.where(lane >= SB_HEAD_DIM, a, zero)


F32_EXP_ZERO = 104.0


def _any_live(runs):
    m = runs[0]
    for r in runs[1:]:
        m = jnp.minimum(m, r)
    return (jnp.min(m) < F32_EXP_ZERO).astype(jnp.int32)


def _sb_heads_block(qs, ks, vs, u, runs, mask, kv_t=False):
    zs = [_dot(q, k) if kv_t else _dot_nt(q, k) for q, k in zip(qs, ks)]
    sps = [jnp.maximum(z, 0.0) + jnp.log(1.0 + jnp.exp(-jnp.abs(z))) for z in zs]
    if mask is not None:
        sps = [jnp.where(mask, sp, 0.0) for sp in sps]
    laters = [_dot(sp.astype(BF16), u) for sp in sps]
    ws = [jnp.exp(((z - sp) - later) - run) for z, sp, later, run in zip(zs, sps, laters, runs)]
    if mask is not None:
        ws = [jnp.where(mask, w, 0.0) for w in ws]
    avs = [_dot_nt(w.astype(BF16), v) if kv_t else _dot(w.astype(BF16), v) for w, v in zip(ws, vs)]
    new_runs = [run + later[:, 0:1] + sp[:, 0:1] for run, later, sp in zip(runs, laters, sps)]
    return avs, new_runs


def _attn_prompt_kernel(q_ref, k_ref, v_ref, o_ref, *, tq, n_lock):
    qi = pl.program_id(2)
    lanes = [slice(p * SB_PAIR, (p + 1) * SB_PAIR) for p in range(n_lock)]
    q_all = q_ref[0]
    qs = [qh for sl in lanes for qh in _split_pair(q_all[:, sl])]
    u = _strict_lower(tq)
    row = lax.broadcasted_iota(jnp.int32, (tq, tq), 0)
    col = lax.broadcasted_iota(jnp.int32, (tq, tq), 1)
    accs = tuple(jnp.zeros((tq, SB_PAIR), F32) for _ in lanes)
    runs = tuple(jnp.zeros((tq, 1), F32) for _ in qs)

    def visit(k0, accs, runs, mask):
        kb = k_ref[0, pl.ds(k0, tq), :]
        vb = v_ref[0, pl.ds(k0, tq), :]
        ks = [kb[:, sl] for sl in lanes for _ in range(2)]
        vs = [vh for sl in lanes for vh in _split_pair(vb[:, sl])]
        avs, runs = _sb_heads_block(qs, ks, vs, u, runs, mask)
        accs = tuple(acc + avs[2 * p] + avs[2 * p + 1] for p, acc in enumerate(accs))
        return accs, tuple(runs)

    accs, runs = visit(pl.multiple_of(qi * tq, tq), accs, runs, col < row)

    def cond(c):
        j, alive, _, _ = c
        return (j >= 0) & (alive > 0)

    def body(c):
        j, _, accs, runs = c
        accs, runs = visit(pl.multiple_of(j * tq, tq), accs, runs, None)
        return j - 1, _any_live(runs), accs, runs

    _, _, accs, _ = lax.while_loop(cond, body, (qi - 1, _any_live(runs), accs, runs))
    o_ref[0] = jnp.concatenate(accs, axis=1).astype(BF16)


def _attn_prompt(q, kb, vb, tq, n_lock):
    B, T, D = q.shape
    width = n_lock * SB_PAIR
    q_spec = pl.BlockSpec((1, tq, width), lambda b, p, i: (b, i, p))
    kv_spec = pl.BlockSpec((1, T, width), lambda b, p, i: (b, 0, p))
    return pl.pallas_call(
        functools.partial(_attn_prompt_kernel, tq=tq, n_lock=n_lock),
        grid=(B, D // width, T // tq),
        in_specs=[q_spec, kv_spec, kv_spec],
        out_specs=q_spec,
        out_shape=jax.ShapeDtypeStruct((B, T, D), BF16),
        compiler_params=pltpu.CompilerParams(
            dimension_semantics=("arbitrary", "arbitrary", "arbitrary"), vmem_limit_bytes=VMEM_LIMIT),
        name="sb_attn_prompt",
    )(q, kb, vb)


def _attn_sample_kernel(q_ref, kn_ref, vn_ref, kc_hbm, vc_hbm, o_ref, kbuf, vbuf, acc_ref, run_ref, sem,
                        *, tq, tk, n_past):
    b = pl.program_id(0)

    def copies(j, slot):
        k0 = pl.multiple_of(j * tk, tk)
        return (pltpu.make_async_copy(kc_hbm.at[b, :, :, pl.ds(k0, tk)], kbuf.at[slot], sem.at[0, slot]),
                pltpu.make_async_copy(vc_hbm.at[b, :, :, pl.ds(k0, tk)], vbuf.at[slot], sem.at[1, slot]))

    def start(j, slot):
        for c in copies(j, slot):
            c.start()

    def wait(j, slot):
        for c in copies(j, slot):
            c.wait()

    j0 = n_past - 1
    start(j0, 0)

    row = lax.broadcasted_iota(jnp.int32, (tq, tq), 0)
    col = lax.broadcasted_iota(jnp.int32, (tq, tq), 1)
    u_new = _strict_lower(tq)
    q = q_ref[0]
    kn = kn_ref[0]
    vn = vn_ref[0]
    head_lanes = [slice(h * SB_HEAD_DIM, (h + 1) * SB_HEAD_DIM) for h in range(SB_HEADS)]
    q_heads = [q[:, sl] for sl in head_lanes]
    avs, runs = _sb_heads_block(q_heads, [kn[:, sl] for sl in head_lanes], [vn[:, sl] for sl in head_lanes],
                                u_new, [jnp.zeros((tq, 1), F32)] * SB_HEADS, col < row)
    for h in range(SB_HEADS):
        acc_ref[h] = avs[h]
        run_ref[h] = runs[h]
    u = _strict_lower(tk)

    def cond(c):
        j, live = c
        return (j >= 0) & (live > 0)

    def body(c):
        j, _ = c
        slot = lax.rem(j0 - j, 2)
        wait(j, slot)

        @pl.when(j > 0)
        def _():
            start(j - 1, 1 - slot)

        ks = [kbuf[slot, h].astype(BF16) for h in range(SB_HEADS)]
        vs = [vbuf[slot, h].astype(BF16) for h in range(SB_HEADS)]
        avs, runs = _sb_heads_block(q_heads, ks, vs, u, [run_ref[h] for h in range(SB_HEADS)], None,
                                    kv_t=True)
        for h in range(SB_HEADS):
            acc_ref[h] += avs[h]
            run_ref[h] = runs[h]
        return j - 1, _any_live(runs)

    jf, _ = lax.while_loop(cond, body, (j0, _any_live(runs)))

    @pl.when(jf >= 0)
    def _():
        wait(jf, lax.rem(j0 - jf, 2))

    o_ref[0] = jnp.concatenate([acc_ref[h] for h in range(SB_HEADS)], axis=1).astype(BF16)


def _attn_sample(q, kb, vb, k_past, v_past, tk):
    B, T, D = q.shape
    n_past = k_past.shape[3] // tk
    q_spec = pl.BlockSpec((1, T, D), lambda b: (b, 0, 0))
    any_spec = pl.BlockSpec(memory_space=pl.ANY)
    return pl.pallas_call(
        functools.partial(_attn_sample_kernel, tq=T, tk=tk, n_past=n_past),
        grid=(B,),
        in_specs=[q_spec, q_spec, q_spec, any_spec, any_spec],
        out_specs=q_spec,
        out_shape=jax.ShapeDtypeStruct((B, T, D), BF16),
        scratch_shapes=[pltpu.VMEM((2, SB_HEADS, SB_HEAD_DIM, tk), F32),
                        pltpu.VMEM((2, SB_HEADS, SB_HEAD_DIM, tk), F32),
                        pltpu.VMEM((SB_HEADS, T, SB_HEAD_DIM), F32),
                        pltpu.VMEM((SB_HEADS, T, 1), F32),
                        pltpu.SemaphoreType.DMA((2, 2))],
        compiler_params=pltpu.CompilerParams(
            dimension_semantics=("arbitrary",), vmem_limit_bytes=VMEM_LIMIT),
        name="sb_attn_sample",
    )(q, kb, vb, k_past, v_past)


def _step_tile(B, T):
    tt = min(T, ROWS_PER_STEP)
    return max(1, min(B, ROWS_PER_STEP // tt)), tt


def _trunk(x, hg_s0, k_past, v_past, conv_s0, p):
    B, T, D = x.shape
    nb_blk, tt = _step_tile(B, T)
    x1, hg_fin = _hgrn_layer(x, hg_s0, p["norm_mix"][0], p["w_in_a"], p["lb_logits"], p["gn_a"],
                             p["w_out_a"], 0, min(nb_blk, HG_STREAMS_PER_STEP), tt)
    x2, conv0 = _ffn_layer(x1, None, None, None if conv_s0 is None else conv_s0[0],
                           p["norm_ffn"][0], p["w_up"][0], p["conv_w"][0], p["conv_b"][0],
                           p["w_down"][0], None, nb_blk, tt)
    feature_major = k_past is None
    q, k, v, kb, vb = _qkv_proj(x2, p["norm_mix"][1], p["w_in_b"], nb_blk, tt, feature_major)
    if k_past is None:
        att = _attn_prompt(q, kb, vb, SB_TILE, SB_LOCKSTEP_PAIRS)
    else:
        att = _attn_sample(q, kb, vb, k_past, v_past, SB_TILE)
    y, conv1 = _ffn_layer(x2, att, p["w_out_b"], None if conv_s0 is None else conv_s0[1],
                          p["norm_ffn"][1], p["w_up"][1], p["conv_w"][1], p["conv_b"][1],
                          p["w_down"][1], p["norm_final"], nb_blk, tt)
    if feature_major:
        to_rows = lambda a: jnp.transpose(a.reshape(B, SB_HEADS, SB_HEAD_DIM, T), (0, 3, 1, 2))[None]
    else:
        to_rows = lambda a: a.reshape(B, T, SB_HEADS, SB_HEAD_DIM)[None]
    return (y, hg_fin[None], to_rows(k), to_rows(v), jnp.stack([conv0, conv1]))


def kernel(x_prompt, x_sample, state_hgrn, cache_k, cache_v, state_conv, norm_mix, norm_ffn, norm_final,
           w_in_a, w_out_a, gn_a, lb_logits, w_in_b, w_out_b, w_up, conv_w, conv_b, w_down):
    row = lambda a: a.reshape(a.shape[:-1] + (1, a.shape[-1]))
    p = {
        "norm_mix": row(norm_mix), "norm_ffn": row(norm_ffn), "norm_final": row(norm_final),
        "w_in_a": w_in_a[0].astype(BF16), "w_out_a": w_out_a[0].astype(BF16),
        "gn_a": gn_a[0].reshape(1, HG_HEADS * HG_DK), "lb_logits": lb_logits,
        "w_in_b": w_in_b[0].astype(BF16), "w_out_b": w_out_b[0].astype(BF16),
        "w_up": [w.astype(BF16) for w in w_up], "conv_w": conv_w, "conv_b": row(conv_b),
        "w_down": [w.astype(BF16) for w in w_down],
    }
    yp, hgp, kp, vp, cp = _trunk(x_prompt, None, None, None, None, p)
    ys, hgs, ks, vs, cs = _trunk(
        x_sample, state_hgrn[0], jnp.transpose(cache_k[0], (0, 2, 3, 1)),
        jnp.transpose(cache_v[0], (0, 2, 3, 1)), state_conv, p)
    return (yp, ys, hgp, kp, vp, cp, hgs, ks, vs, cs)
```

```python
import functools

import jax
import jax.numpy as jnp
from jax import lax
from jax.experimental import pallas as pl
from jax.experimental.pallas import tpu as pltpu

F32 = jnp.float32
BF16 = jnp.bfloat16

D_MODEL = 1024
CHUNK = 64
HG_HEADS = 8
HG_DK = 128
HG_UNIT_ROWS = 256
HG_STREAMS_PER_STEP = 4
SB_HEADS = 16
SB_HEAD_DIM = 64
SB_PAIR = 2 * SB_HEAD_DIM
D_FF = 2816
FF_COLS = 256
CONV_WIDTH = 3
EPS = 1e-6
VMEM_LIMIT = 56 * 1024 * 1024
ROWS_PER_STEP = 512
SB_TILE = 256
SB_LOCKSTEP_PAIRS = 4

_NT = (((1,), (1,)), ((), ()))


def _dot(a, b):
    return jnp.dot(a, b, preferred_element_type=F32)


def _dot_nt(a, b):
    return lax.dot_general(a, b, _NT, preferred_element_type=F32)


def _rms(x, g):
    r = lax.rsqrt(jnp.mean(x * x, axis=-1, keepdims=True) + EPS)
    return x * r * g


def _layer_spec(shape, layer):
    nd = len(shape) - 1
    return pl.BlockSpec((None,) + tuple(shape[1:]), lambda *_: (layer,) + (0,) * nd,
                        pipeline_mode=pl.Buffered(1))


def _const_spec(shape):
    nd = len(shape)
    return pl.BlockSpec(shape, lambda *_: (0,) * nd, pipeline_mode=pl.Buffered(1))


def _hgrn_kernel(*refs, nb_blk, tt, has_s0, n_lb_rows):
    if has_s0:
        (x_ref, s0_ref, g_ref, win_ref, lbl_ref, gn_ref, wout_ref,
         o_ref, sfin_ref, st_ref) = refs
    else:
        (x_ref, g_ref, win_ref, lbl_ref, gn_ref, wout_ref,
         o_ref, sfin_ref, st_ref) = refs
        s0_ref = None
    t = pl.program_id(1)
    m = nb_blk * tt
    ur = min(m, HG_UNIT_ROWS)
    kd = HG_HEADS * HG_DK

    @pl.when(t == 0)
    def _():
        if has_s0:
            for nb in range(nb_blk):
                for h in range(HG_HEADS):
                    st_ref[nb, h] = s0_ref[nb, h].T
        else:
            st_ref[...] = jnp.zeros_like(st_ref)

    ll = lbl_ref[...]
    e = jnp.exp(ll - jnp.max(ll, axis=0, keepdims=True))
    sm = e / jnp.sum(e, axis=0, keepdims=True)
    lb = jnp.sum(sm[:n_lb_rows], axis=0, keepdims=True)

    ri = lax.broadcasted_iota(jnp.int32, (ur, ur), 0)
    ci = lax.broadcasted_iota(jnp.int32, (ur, ur), 1)
    tri = jnp.where(((ri >> 6) == (ci >> 6)) & (ci <= ri), 1.0, 0.0).astype(BF16)
    gn = gn_ref[...]
    t64 = lax.broadcasted_iota(jnp.int32, (CHUNK, CHUNK), 0)
    s64 = lax.broadcasted_iota(jnp.int32, (CHUNK, CHUNK), 1)
    causal = s64 <= t64
    head_lanes = [slice(h * HG_DK, (h + 1) * HG_DK) for h in range(HG_HEADS)]
    x_all = x_ref[...].reshape(m, D_MODEL)

    def project(u0):
        x = x_all[u0:u0 + ur]
        hn = _rms(x, g_ref[...]).astype(BF16)
        proj = _dot(hn, win_ref[...])
        q = proj[:, :kd]
        fpre = proj[:, kd:2 * kd]
        v = proj[:, 2 * kd:3 * kd]
        gate = proj[:, 3 * kd:]
        en = jnp.exp(-jnp.abs(fpre))
        r = 1.0 / (1.0 + en)
        er = en * r
        pos = fpre >= 0.0
        logf = jnp.log(lb + (1.0 - lb) * jnp.where(pos, r, er))
        k = (1.0 - lb) * jnp.where(pos, er, r)
        hi = logf.astype(BF16)
        r1 = logf - hi.astype(F32)
        mid = r1.astype(BF16)
        lo = (r1 - mid.astype(F32)).astype(BF16)
        b = _dot(tri, hi) + _dot(tri, mid) + _dot(tri, lo)
        return x, q, k, v, b, jax.nn.sigmoid(gate)

    units = list(range(0, m, ur))
    projected = [project(u0) for u0 in units]
    sts = {}
    for u0, (x, q, k, v, b, sgate) in zip(units, projected):
        ogs = []
        for c in range(ur // CHUNK):
            nb = (u0 + c * CHUNK) // tt
            if nb not in sts:
                sts[nb] = [st_ref[nb, h] for h in range(HG_HEADS)]
            rows = slice(c * CHUNK, (c + 1) * CHUNK)
            bc = b[rows]
            bl = bc[CHUNK - 1:CHUNK]
            qt = (q[rows] * jnp.exp(bc)).astype(BF16)
            kt = (k[rows] * jnp.exp(-bc)).astype(BF16)
            ke = (k[rows] * jnp.exp(bl - bc)).astype(BF16)
            ebl = jnp.exp(bl)
            vc = v[rows]
            attns = [jnp.where(causal, _dot_nt(qt[:, sl], kt[:, sl]), 0.0) for sl in head_lanes]
            inters = [_dot_nt(qt[:, sl], st.astype(BF16)) for sl, st in zip(head_lanes, sts[nb])]
            upds = [_dot(vc[:, sl].T.astype(BF16), ke[:, sl]) for sl in head_lanes]
            outs = [_dot(attn.astype(BF16), vc[:, sl].astype(BF16)) + inter
                    for attn, sl, inter in zip(attns, head_lanes, inters)]
            sts[nb] = [st * ebl[:, sl] + upd for st, sl, upd in zip(sts[nb], head_lanes, upds)]
            ons = [o * lax.rsqrt(jnp.mean(o * o, axis=-1, keepdims=True) + EPS) for o in outs]
            ogs.append((jnp.concatenate(ons, axis=1) * gn * sgate[rows]).astype(BF16))
        out = x + _dot(jnp.concatenate(ogs, axis=0), wout_ref[...])
        if tt >= ur:
            o_ref[u0 // tt, u0 % tt:u0 % tt + ur, :] = out
        else:
            o_ref[u0 // tt:(u0 + ur) // tt] = out.reshape(ur // tt, tt, D_MODEL)
    for nb, st in sts.items():
        for h in range(HG_HEADS):
            st_ref[nb, h] = st[h]

    @pl.when(t == pl.num_programs(1) - 1)
    def _():
        for nb in range(nb_blk):
            for h in range(HG_HEADS):
                sfin_ref[nb, h] = st_ref[nb, h].T


def _hgrn_layer(x, s0, norm_g, w_in, lb_logits, gn, w_out, layer_idx, nb_blk, tt):
    B, T, D = x.shape
    has_s0 = s0 is not None
    grid = (B // nb_blk, T // tt)
    x_spec = pl.BlockSpec((nb_blk, tt, D), lambda b, t: (b, t, 0))
    st_spec = pl.BlockSpec((nb_blk, HG_HEADS, HG_DK, HG_DK), lambda b, t: (b, 0, 0, 0))
    in_specs = [x_spec]
    args = [x]
    if has_s0:
        in_specs.append(st_spec)
        args.append(s0)
    in_specs += [_const_spec(norm_g.shape), _const_spec(w_in.shape), _const_spec(lb_logits.shape),
                 _const_spec(gn.shape), _const_spec(w_out.shape)]
    args += [norm_g, w_in, lb_logits, gn, w_out]
    return pl.pallas_call(
        functools.partial(_hgrn_kernel, nb_blk=nb_blk, tt=tt, has_s0=has_s0, n_lb_rows=layer_idx + 1),
        grid=grid,
        in_specs=in_specs,
        out_specs=[x_spec, st_spec],
        out_shape=[jax.ShapeDtypeStruct((B, T, D), F32),
                   jax.ShapeDtypeStruct((B, HG_HEADS, HG_DK, HG_DK), F32)],
        scratch_shapes=[pltpu.VMEM((nb_blk, HG_HEADS, HG_DK, HG_DK), F32)],
        compiler_params=pltpu.CompilerParams(
            dimension_semantics=("arbitrary", "arbitrary"), vmem_limit_bytes=VMEM_LIMIT),
        name="hgrn_layer",
    )(*args)


def _ffn_kernel(*refs, nb_blk, tt, has_s0, has_pre, final_norm):
    refs = list(refs)
    x_ref = refs.pop(0)
    att_ref = refs.pop(0) if has_pre else None
    wo_ref = refs.pop(0) if has_pre else None
    s0_ref = refs.pop(0) if has_s0 else None
    g_ref, wup_ref, cw_ref, cb_ref, wdn_ref = refs[:5]
    refs = refs[5:]
    gf_ref = refs.pop(0) if final_norm else None
    y_ref, cst_ref, carry_ref, hmid_ref = refs
    t = pl.program_id(1)
    m = nb_blk * tt

    @pl.when(t == 0)
    def _():
        if has_s0:
            carry_ref[...] = s0_ref[...]
        else:
            carry_ref[...] = jnp.zeros_like(carry_ref)

    x = x_ref[...].reshape(m, D_MODEL)
    if has_pre:
        x = x + _dot(att_ref[...].reshape(m, D_MODEL), wo_ref[...])
    hn = _rms(x, g_ref[...]).astype(BF16)
    up = _dot(hn, wup_ref[...])
    row = lax.broadcasted_iota(jnp.int32, (tt, FF_COLS), 0)
    prev = carry_ref[...]

    for j in range(D_FF // FF_COLS):
        cols = slice(j * FF_COLS, (j + 1) * FF_COLS)
        a = up[:, cols]
        g = up[:, D_FF + j * FF_COLS:D_FF + (j + 1) * FF_COLS]
        w0 = cw_ref[0:1, cols]
        w1 = cw_ref[1:2, cols]
        w2 = cw_ref[2:3, cols]
        bias = cb_ref[:, cols]
        for nb in range(nb_blk):
            rows = slice(nb * tt, (nb + 1) * tt)
            an = a[rows]
            c0 = prev[nb, 0:1, cols]
            c1 = prev[nb, 1:2, cols]
            am1 = jnp.where(row == 0, c1, pltpu.roll(an, 1, axis=0))
            am2 = jnp.where(row == 0, c0, jnp.where(row == 1, c1, pltpu.roll(an, 2, axis=0)))
            carry_ref[nb, :, cols] = an[tt - 2:tt]
            conv = bias + w0 * am2 + w1 * am1 + w2 * an
            hmid_ref[rows, cols] = (jax.nn.gelu(conv) * g[rows]).astype(BF16)

    out = x + _dot(hmid_ref[...], wdn_ref[...])
    if final_norm:
        out = _rms(out, gf_ref[...])
    y_ref[...] = out.reshape(nb_blk, tt, D_MODEL)

    @pl.when(t == pl.num_programs(1) - 1)
    def _():
        cst_ref[...] = carry_ref[...]


def _ffn_layer(x, att, w_o, s0, norm_g, w_up, conv_w, conv_b, w_down, norm_final, layer, nb_blk, tt):
    B, T, D = x.shape
    has_pre = att is not None
    has_s0 = s0 is not None
    final_norm = norm_final is not None
    grid = (B // nb_blk, T // tt)
    x_spec = pl.BlockSpec((nb_blk, tt, D), lambda b, t: (b, t, 0))
    cs_spec = pl.BlockSpec((nb_blk, CONV_WIDTH - 1, D_FF), lambda b, t: (b, 0, 0))
    in_specs = [x_spec]
    args = [x]
    if has_pre:
        in_specs += [x_spec, _const_spec(w_o.shape)]
        args += [att, w_o]
    if has_s0:
        in_specs.append(cs_spec)
        args.append(s0)
    in_specs += [_const_spec(norm_g.shape), _layer_spec(w_up.shape, layer), _const_spec(conv_w.shape),
                 _const_spec(conv_b.shape), _layer_spec(w_down.shape, layer)]
    args += [norm_g, w_up, conv_w, conv_b, w_down]
    if final_norm:
        in_specs.append(_const_spec(norm_final.shape))
        args.append(norm_final)
    return pl.pallas_call(
        functools.partial(_ffn_kernel, nb_blk=nb_blk, tt=tt, has_s0=has_s0, has_pre=has_pre,
                          final_norm=final_norm),
        grid=grid,
        in_specs=in_specs,
        out_specs=[x_spec, cs_spec],
        out_shape=[jax.ShapeDtypeStruct((B, T, D), F32),
                   jax.ShapeDtypeStruct((B, CONV_WIDTH - 1, D_FF), F32)],
        scratch_shapes=[pltpu.VMEM((nb_blk, CONV_WIDTH - 1, D_FF), F32),
                        pltpu.VMEM((nb_blk * tt, D_FF), BF16)],
        compiler_params=pltpu.CompilerParams(
            dimension_semantics=("arbitrary", "arbitrary"), vmem_limit_bytes=VMEM_LIMIT),
        name="conv_ffn",
    )(*args)


def _qkv_kernel(x_ref, g_ref, w_ref, q_ref, k_ref, v_ref, kb_ref, vb_ref, *, nb_blk, tt, feature_major):
    m = nb_blk * tt
    shp = (nb_blk, tt, D_MODEL)
    hn = _rms(x_ref[...].reshape(m, D_MODEL), g_ref[...]).astype(BF16)
    proj = _dot(hn, w_ref[...])
    q_ref[...] = (proj[:, :D_MODEL] * (SB_HEAD_DIM ** -0.5)).astype(BF16).reshape(shp)
    k = proj[:, D_MODEL:2 * D_MODEL]
    v = proj[:, 2 * D_MODEL:]
    if feature_major:
        for nb in range(nb_blk):
            k_ref[nb] = k[nb * tt:(nb + 1) * tt].T
            v_ref[nb] = v[nb * tt:(nb + 1) * tt].T
    else:
        k_ref[...] = k.reshape(shp)
        v_ref[...] = v.reshape(shp)
    kb_ref[...] = k.astype(BF16).reshape(shp)
    vb_ref[...] = v.astype(BF16).reshape(shp)


def _qkv_proj(x, norm_g, w_in, nb_blk, tt, feature_major):
    B, T, D = x.shape
    x_spec = pl.BlockSpec((nb_blk, tt, D), lambda b, t: (b, t, 0))
    kv_spec = pl.BlockSpec((nb_blk, D, tt), lambda b, t: (b, 0, t)) if feature_major else x_spec
    kv_shape = (B, D, T) if feature_major else (B, T, D)
    return pl.pallas_call(
        functools.partial(_qkv_kernel, nb_blk=nb_blk, tt=tt, feature_major=feature_major),
        grid=(B // nb_blk, T // tt),
        in_specs=[x_spec, _const_spec(norm_g.shape), _const_spec(w_in.shape)],
        out_specs=[x_spec, kv_spec, kv_spec, x_spec, x_spec],
        out_shape=[jax.ShapeDtypeStruct((B, T, D), BF16),
                   jax.ShapeDtypeStruct(kv_shape, F32),
                   jax.ShapeDtypeStruct(kv_shape, F32),
                   jax.ShapeDtypeStruct((B, T, D), BF16),
                   jax.ShapeDtypeStruct((B, T, D), BF16)],
        compiler_params=pltpu.CompilerParams(
            dimension_semantics=("arbitrary", "arbitrary"), vmem_limit_bytes=VMEM_LIMIT),
        name="sb_qkv",
    )(x, norm_g, w_in)


def _strict_lower(n):
    j = lax.broadcasted_iota(jnp.int32, (n, n), 0)
    s = lax.broadcasted_iota(jnp.int32, (n, n), 1)
    return jnp.where(j > s, 1.0, 0.0).astype(BF16)


def _split_pair(a):
    lane = lax.broadcasted_iota(jnp.int32, a.shape, 1)
    zero = jnp.zeros_like(a)
    return jnp.where(lane < SB_HEAD_DIM, a, zero), jnp.where(lane >= SB_HEAD_DIM, a, zero)


F32_EXP_ZERO = 104.0


def _any_live(runs):
    m = runs[0]
    for r in runs[1:]:
        m = jnp.minimum(m, r)
    return (jnp.min(m) < F32_EXP_ZERO).astype(jnp.int32)


def _sb_heads_blocks(qs, kss, vss, us, runs, masks, kv_t=False):
    zs = [[_dot(q, k) if kv_t else _dot_nt(q, k) for q, k in zip(qs, ks)] for ks in kss]
    sps = [[jnp.maximum(z, 0.0) + jnp.log(1.0 + jnp.exp(-jnp.abs(z))) for z in zb] for zb in zs]
    sps = [spb if m is None else [jnp.where(m, sp, 0.0) for sp in spb] for spb, m in zip(sps, masks)]
    laters = [[_dot(sp.astype(BF16), u) for sp in spb] for spb, u in zip(sps, us)]
    avs = None
    for zb, spb, lb, vs, m in zip(zs, sps, laters, vss, masks):
        ws = [jnp.exp(((z - sp) - later) - run) for z, sp, later, run in zip(zb, spb, lb, runs)]
        if m is not None:
            ws = [jnp.where(m, w, 0.0) for w in ws]
        av = [_dot_nt(w.astype(BF16), v) if kv_t else _dot(w.astype(BF16), v) for w, v in zip(ws, vs)]
        avs = av if avs is None else [a + b for a, b in zip(avs, av)]
        runs = [run + later[:, 0:1] + sp[:, 0:1] for run, later, sp in zip(runs, lb, spb)]
    return avs, runs


def _attn_prompt_kernel(q_ref, k_ref, v_ref, o_ref, *, tq, n_lock):
    qi = pl.program_id(2)
    lanes = [slice(p * SB_PAIR, (p + 1) * SB_PAIR) for p in range(n_lock)]
    q_all = q_ref[0]
    qs = [qh for sl in lanes for qh in _split_pair(q_all[:, sl])]
    u = _strict_lower(tq)
    row = lax.broadcasted_iota(jnp.int32, (tq, tq), 0)
    col = lax.broadcasted_iota(jnp.int32, (tq, tq), 1)
    accs = tuple(jnp.zeros((tq, SB_PAIR), F32) for _ in lanes)
    runs = tuple(jnp.zeros((tq, 1), F32) for _ in qs)

    def visit(blocks, accs, runs, masks):
        kss, vss = [], []
        for j in blocks:
            k0 = pl.multiple_of(j * tq, tq)
            kb = k_ref[0, pl.ds(k0, tq), :]
            vb = v_ref[0, pl.ds(k0, tq), :]
            kss.append([kb[:, sl] for sl in lanes for _ in range(2)])
            vss.append([vh for sl in lanes for vh in _split_pair(vb[:, sl])])
        avs, runs = _sb_heads_blocks(qs, kss, vss, [u] * len(blocks), runs, masks)
        accs = tuple(acc + avs[2 * p] + avs[2 * p + 1] for p, acc in enumerate(accs))
        return accs, tuple(runs)

    def finish(j, accs, runs):
        def cond(c):
            j, alive, _, _ = c
            return (j >= 0) & (alive > 0)

        def body(c):
            j, _, accs, runs = c
            accs, runs = visit([j], accs, runs, [None])
            return j - 1, _any_live(runs), accs, runs

        _, _, accs, _ = lax.while_loop(cond, body, (j, _any_live(runs), accs, runs))
        o_ref[0] = jnp.concatenate(accs, axis=1).astype(BF16)

    @pl.when(qi == 0)
    def _():
        finish(qi - 1, *visit([qi], accs, runs, [col < row]))

    @pl.when(qi > 0)
    def _():
        finish(qi - 2, *visit([qi, qi - 1], accs, runs, [col < row, None]))


def _attn_prompt(q, kb, vb, tq, n_lock):
    B, T, D = q.shape
    width = n_lock * SB_PAIR
    q_spec = pl.BlockSpec((1, tq, width), lambda b, p, i: (b, i, p))
    kv_spec = pl.BlockSpec((1, T, width), lambda b, p, i: (b, 0, p))
    return pl.pallas_call(
        functools.partial(_attn_prompt_kernel, tq=tq, n_lock=n_lock),
        grid=(B, D // width, T // tq),
        in_specs=[q_spec, kv_spec, kv_spec],
        out_specs=q_spec,
        out_shape=jax.ShapeDtypeStruct((B, T, D), BF16),
        compiler_params=pltpu.CompilerParams(
            dimension_semantics=("arbitrary", "arbitrary", "arbitrary"), vmem_limit_bytes=VMEM_LIMIT),
        name="sb_attn_prompt",
    )(q, kb, vb)


def _attn_sample_kernel(q_ref, kn_ref, vn_ref, kc_hbm, vc_hbm, o_ref, kbuf, vbuf, acc_ref, run_ref, sem,
                        *, tq, tk, n_past):
    b = pl.program_id(0)

    def copies(j, slot):
        k0 = pl.multiple_of(j * tk, tk)
        return (pltpu.make_async_copy(kc_hbm.at[b, :, :, pl.ds(k0, tk)], kbuf.at[slot], sem.at[0, slot]),
                pltpu.make_async_copy(vc_hbm.at[b, :, :, pl.ds(k0, tk)], vbuf.at[slot], sem.at[1, slot]))

    def start(j, slot):
        for c in copies(j, slot):
            c.start()

    def wait(j, slot):
        for c in copies(j, slot):
            c.wait()

    j0 = n_past - 1
    start(j0, 0)

    row = lax.broadcasted_iota(jnp.int32, (tq, tq), 0)
    col = lax.broadcasted_iota(jnp.int32, (tq, tq), 1)
    u_new = _strict_lower(tq)
    q = q_ref[0]
    kn = kn_ref[0]
    vn = vn_ref[0]
    head_lanes = [slice(h * SB_HEAD_DIM, (h + 1) * SB_HEAD_DIM) for h in range(SB_HEADS)]
    q_heads = [q[:, sl] for sl in head_lanes]
    avs, runs = _sb_heads_blocks(q_heads, [[kn[:, sl] for sl in head_lanes]],
                                 [[vn[:, sl] for sl in head_lanes]], [u_new],
                                 [jnp.zeros((tq, 1), F32)] * SB_HEADS, [col < row])
    for h in range(SB_HEADS):
        acc_ref[h] = avs[h]
        run_ref[h] = runs[h]
    u = _strict_lower(tk)

    def cond(c):
        j, live = c
        return (j >= 0) & (live > 0)

    def body(c):
        j, _ = c
        slot = lax.rem(j0 - j, 2)
        wait(j, slot)

        @pl.when(j > 0)
        def _():
            start(j - 1, 1 - slot)

        ks = [kbuf[slot, h].astype(BF16) for h in range(SB_HEADS)]
        vs = [vbuf[slot, h].astype(BF16) for h in range(SB_HEADS)]
        avs, runs = _sb_heads_blocks(q_heads, [ks], [vs], [u], [run_ref[h] for h in range(SB_HEADS)],
                                     [None], kv_t=True)
        for h in range(SB_HEADS):
            acc_ref[h] += avs[h]
            run_ref[h] = runs[h]
        return j - 1, _any_live(runs)

    jf, _ = lax.while_loop(cond, body, (j0, _any_live(runs)))

    @pl.when(jf >= 0)
    def _():
        wait(jf, lax.rem(j0 - jf, 2))

    o_ref[0] = jnp.concatenate([acc_ref[h] for h in range(SB_HEADS)], axis=1).astype(BF16)


def _attn_sample(q, kb, vb, k_past, v_past, tk):
    B, T, D = q.shape
    n_past = k_past.shape[3] // tk
    q_spec = pl.BlockSpec((1, T, D), lambda b: (b, 0, 0))
    any_spec = pl.BlockSpec(memory_space=pl.ANY)
    return pl.pallas_call(
        functools.partial(_attn_sample_kernel, tq=T, tk=tk, n_past=n_past),
        grid=(B,),
        in_specs=[q_spec, q_spec, q_spec, any_spec, any_spec],
        out_specs=q_spec,
        out_shape=jax.ShapeDtypeStruct((B, T, D), BF16),
        scratch_shapes=[pltpu.VMEM((2, SB_HEADS, SB_HEAD_DIM, tk), F32),
                        pltpu.VMEM((2, SB_HEADS, SB_HEAD_DIM, tk), F32),
                        pltpu.VMEM((SB_HEADS, T, SB_HEAD_DIM), F32),
                        pltpu.VMEM((SB_HEADS, T, 1), F32),
                        pltpu.SemaphoreType.DMA((2, 2))],
        compiler_params=pltpu.CompilerParams(
            dimension_semantics=("arbitrary",), vmem_limit_bytes=VMEM_LIMIT),
        name="sb_attn_sample",
    )(q, kb, vb, k_past, v_past)


def _step_tile(B, T):
    tt = min(T, ROWS_PER_STEP)
    return max(1, min(B, ROWS_PER_STEP // tt)), tt


def _trunk(x, hg_s0, k_past, v_past, conv_s0, p):
    B, T, D = x.shape
    nb_blk, tt = _step_tile(B, T)
    x1, hg_fin = _hgrn_layer(x, hg_s0, p["norm_mix"][0], p["w_in_a"], p["lb_logits"], p["gn_a"],
                             p["w_out_a"], 0, min(nb_blk, HG_STREAMS_PER_STEP), tt)
    x2, conv0 = _ffn_layer(x1, None, None, None if conv_s0 is None else conv_s0[0],
                           p["norm_ffn"][0], p["w_up"], p["conv_w"][0], p["conv_b"][0],
                           p["w_down"], None, 0, nb_blk, tt)
    feature_major = k_past is None
    q, k, v, kb, vb = _qkv_proj(x2, p["norm_mix"][1], p["w_in_b"], nb_blk, tt, feature_major)
    if k_past is None:
        att = _attn_prompt(q, kb, vb, SB_TILE, SB_LOCKSTEP_PAIRS)
    else:
        att = _attn_sample(q, kb, vb, k_past, v_past, SB_TILE)
    y, conv1 = _ffn_layer(x2, att, p["w_out_b"], None if conv_s0 is None else conv_s0[1],
                          p["norm_ffn"][1], p["w_up"], p["conv_w"][1], p["conv_b"][1],
                          p["w_down"], p["norm_final"], 1, nb_blk, tt)
    if feature_major:
        to_rows = lambda a: jnp.transpose(a.reshape(B, SB_HEADS, SB_HEAD_DIM, T), (0, 3, 1, 2))[None]
    else:
        to_rows = lambda a: a.reshape(B, T, SB_HEADS, SB_HEAD_DIM)[None]
    return (y, hg_fin[None], to_rows(k), to_rows(v), jnp.stack([conv0, conv1]))


def kernel(x_prompt, x_sample, state_hgrn, cache_k, cache_v, state_conv, norm_mix, norm_ffn, norm_final,
           w_in_a, w_out_a, gn_a, lb_logits, w_in_b, w_out_b, w_up, conv_w, conv_b, w_down):
    row = lambda a: a.reshape(a.shape[:-1] + (1, a.shape[-1]))
    p = {
        "norm_mix": row(norm_mix), "norm_ffn": row(norm_ffn), "norm_final": row(norm_final),
        "w_in_a": w_in_a[0].astype(BF16), "w_out_a": w_out_a[0].astype(BF16),
        "gn_a": gn_a[0].reshape(1, HG_HEADS * HG_DK), "lb_logits": lb_logits,
        "w_in_b": w_in_b[0].astype(BF16), "w_out_b": w_out_b[0].astype(BF16),
        "w_up": w_up.astype(BF16), "conv_w": conv_w, "conv_b": row(conv_b),
        "w_down": w_down.astype(BF16),
    }
    yp, hgp, kp, vp, cp = _trunk(x_prompt, None, None, None, None, p)
    ys, hgs, ks, vs, cs = _trunk(
        x_sample, state_hgrn[0], jnp.transpose(cache_k[0], (0, 2, 3, 1)),
        jnp.transpose(cache_v[0], (0, 2, 3, 1)), state_conv, p)
    return (yp, ys, hgp, kp, vp, cp, hgs, ks, vs, cs)
```

```python
import functools

import jax
import jax.numpy as jnp
from jax import lax
from jax.experimental import pallas as pl
from jax.experimental.pallas import tpu as pltpu

F32 = jnp.float32
BF16 = jnp.bfloat16

D_MODEL = 1024
CHUNK = 64
HG_HEADS = 8
HG_DK = 128
HG_UNIT_ROWS = 256
HG_STREAMS_PER_STEP = 4
SB_HEADS = 16
SB_HEAD_DIM = 64
SB_PAIR = 2 * SB_HEAD_DIM
D_FF = 2816
FF_COLS = 256
CONV_WIDTH = 3
EPS = 1e-6
VMEM_LIMIT = 56 * 1024 * 1024
ROWS_PER_STEP = 512
SB_TILE = 256
SB_LOCKSTEP_PAIRS = 4

_NT = (((1,), (1,)), ((), ()))


def _dot(a, b):
    return jnp.dot(a, b, preferred_element_type=F32)


def _dot_nt(a, b):
    return lax.dot_general(a, b, _NT, preferred_element_type=F32)


def _rms(x, g):
    r = lax.rsqrt(jnp.mean(x * x, axis=-1, keepdims=True) + EPS)
    return x * r * g


def _layer_spec(shape, layer):
    nd = len(shape) - 1
    return pl.BlockSpec((None,) + tuple(shape[1:]), lambda *_: (layer,) + (0,) * nd,
                        pipeline_mode=pl.Buffered(1))


def _const_spec(shape):
    nd = len(shape)
    return pl.BlockSpec(shape, lambda *_: (0,) * nd, pipeline_mode=pl.Buffered(1))


def _hgrn_kernel(*refs, nb_blk, tt, has_s0, n_lb_rows):
    if has_s0:
        (x_ref, s0_ref, g_ref, win_ref, lbl_ref, gn_ref, wout_ref,
         o_ref, sfin_ref, st_ref) = refs
    else:
        (x_ref, g_ref, win_ref, lbl_ref, gn_ref, wout_ref,
         o_ref, sfin_ref, st_ref) = refs
        s0_ref = None
    t = pl.program_id(1)
    m = nb_blk * tt
    ur = min(m, HG_UNIT_ROWS)
    kd = HG_HEADS * HG_DK

    @pl.when(t == 0)
    def _():
        if has_s0:
            for nb in range(nb_blk):
                for h in range(HG_HEADS):
                    st_ref[nb, h] = s0_ref[nb, h].T
        else:
            st_ref[...] = jnp.zeros_like(st_ref)

    ll = lbl_ref[...]
    e = jnp.exp(ll - jnp.max(ll, axis=0, keepdims=True))
    sm = e / jnp.sum(e, axis=0, keepdims=True)
    lb = jnp.sum(sm[:n_lb_rows], axis=0, keepdims=True)

    ri = lax.broadcasted_iota(jnp.int32, (ur, ur), 0)
    ci = lax.broadcasted_iota(jnp.int32, (ur, ur), 1)
    tri = jnp.where(((ri >> 6) == (ci >> 6)) & (ci <= ri), 1.0, 0.0).astype(BF16)
    gn = gn_ref[...]
    t64 = lax.broadcasted_iota(jnp.int32, (CHUNK, CHUNK), 0)
    s64 = lax.broadcasted_iota(jnp.int32, (CHUNK, CHUNK), 1)
    causal = s64 <= t64
    head_lanes = [slice(h * HG_DK, (h + 1) * HG_DK) for h in range(HG_HEADS)]
    x_all = x_ref[...].reshape(m, D_MODEL)

    def project(u0):
        x = x_all[u0:u0 + ur]
        hn = _rms(x, g_ref[...]).astype(BF16)
        proj = _dot(hn, win_ref[...])
        q = proj[:, :kd]
        fpre = proj[:, kd:2 * kd]
        v = proj[:, 2 * kd:3 * kd]
        gate = proj[:, 3 * kd:]
        en = jnp.exp(-jnp.abs(fpre))
        r = 1.0 / (1.0 + en)
        er = en * r
        pos = fpre >= 0.0
        logf = jnp.log(lb + (1.0 - lb) * jnp.where(pos, r, er))
        k = (1.0 - lb) * jnp.where(pos, er, r)
        hi = logf.astype(BF16)
        lo = (logf - hi.astype(F32)).astype(BF16)
        b = _dot(tri, hi) + _dot(tri, lo)
        return x, q, k, v, b, jax.nn.sigmoid(gate)

    units = list(range(0, m, ur))
    projected = [project(u0) for u0 in units]
    sts = {}
    for u0, (x, q, k, v, b, sgate) in zip(units, projected):
        ogs = []
        for c in range(ur // CHUNK):
            nb = (u0 + c * CHUNK) // tt
            if nb not in sts:
                sts[nb] = [st_ref[nb, h] for h in range(HG_HEADS)]
            rows = slice(c * CHUNK, (c + 1) * CHUNK)
            bc = b[rows]
            bl = bc[CHUNK - 1:CHUNK]
            qt = (q[rows] * jnp.exp(bc)).astype(BF16)
            kt = (k[rows] * jnp.exp(-bc)).astype(BF16)
            ke = (k[rows] * jnp.exp(bl - bc)).astype(BF16)
            ebl = jnp.exp(bl)
            vc = v[rows]
            attns = [jnp.where(causal, _dot_nt(qt[:, sl], kt[:, sl]), 0.0) for sl in head_lanes]
            inters = [_dot_nt(qt[:, sl], st.astype(BF16)) for sl, st in zip(head_lanes, sts[nb])]
            upds = [_dot(vc[:, sl].T.astype(BF16), ke[:, sl]) for sl in head_lanes]
            outs = [_dot(attn.astype(BF16), vc[:, sl].astype(BF16)) + inter
                    for attn, sl, inter in zip(attns, head_lanes, inters)]
            sts[nb] = [st * ebl[:, sl] + upd for st, sl, upd in zip(sts[nb], head_lanes, upds)]
            ons = [o * lax.rsqrt(jnp.mean(o * o, axis=-1, keepdims=True) + EPS) for o in outs]
            ogs.append((jnp.concatenate(ons, axis=1) * gn * sgate[rows]).astype(BF16))
        out = x + _dot(jnp.concatenate(ogs, axis=0), wout_ref[...])
        if tt >= ur:
            o_ref[u0 // tt, u0 % tt:u0 % tt + ur, :] = out
        else:
            o_ref[u0 // tt:(u0 + ur) // tt] = out.reshape(ur // tt, tt, D_MODEL)
    for nb, st in sts.items():
        for h in range(HG_HEADS):
            st_ref[nb, h] = st[h]

    @pl.when(t == pl.num_programs(1) - 1)
    def _():
        for nb in range(nb_blk):
            for h in range(HG_HEADS):
                sfin_ref[nb, h] = st_ref[nb, h].T


def _hgrn_layer(x, s0, norm_g, w_in, lb_logits, gn, w_out, layer_idx, nb_blk, tt):
    B, T, D = x.shape
    has_s0 = s0 is not None
    grid = (B // nb_blk, T // tt)
    x_spec = pl.BlockSpec((nb_blk, tt, D), lambda b, t: (b, t, 0))
    st_spec = pl.BlockSpec((nb_blk, HG_HEADS, HG_DK, HG_DK), lambda b, t: (b, 0, 0, 0))
    in_specs = [x_spec]
    args = [x]
    if has_s0:
        in_specs.append(st_spec)
        args.append(s0)
    in_specs += [_const_spec(norm_g.shape), _const_spec(w_in.shape), _const_spec(lb_logits.shape),
                 _const_spec(gn.shape), _const_spec(w_out.shape)]
    args += [norm_g, w_in, lb_logits, gn, w_out]
    return pl.pallas_call(
        functools.partial(_hgrn_kernel, nb_blk=nb_blk, tt=tt, has_s0=has_s0, n_lb_rows=layer_idx + 1),
        grid=grid,
        in_specs=in_specs,
        out_specs=[x_spec, st_spec],
        out_shape=[jax.ShapeDtypeStruct((B, T, D), F32),
                   jax.ShapeDtypeStruct((B, HG_HEADS, HG_DK, HG_DK), F32)],
        scratch_shapes=[pltpu.VMEM((nb_blk, HG_HEADS, HG_DK, HG_DK), F32)],
        compiler_params=pltpu.CompilerParams(
            dimension_semantics=("arbitrary", "arbitrary"), vmem_limit_bytes=VMEM_LIMIT),
        name="hgrn_layer",
    )(*args)


def _ffn_kernel(*refs, nb_blk, tt, has_s0, has_pre, final_norm):
    refs = list(refs)
    x_ref = refs.pop(0)
    att_ref = refs.pop(0) if has_pre else None
    wo_ref = refs.pop(0) if has_pre else None
    s0_ref = refs.pop(0) if has_s0 else None
    g_ref, wup_ref, cw_ref, cb_ref, wdn_ref = refs[:5]
    refs = refs[5:]
    gf_ref = refs.pop(0) if final_norm else None
    y_ref, cst_ref, carry_ref, hmid_ref = refs
    t = pl.program_id(1)
    m = nb_blk * tt

    @pl.when(t == 0)
    def _():
        if has_s0:
            carry_ref[...] = s0_ref[...]
        else:
            carry_ref[...] = jnp.zeros_like(carry_ref)

    x = x_ref[...].reshape(m, D_MODEL)
    if has_pre:
        x = x + _dot(att_ref[...].reshape(m, D_MODEL), wo_ref[...])
    hn = _rms(x, g_ref[...]).astype(BF16)
    up = _dot(hn, wup_ref[...])
    row = lax.broadcasted_iota(jnp.int32, (tt, FF_COLS), 0)
    prev = carry_ref[...]

    for j in range(D_FF // FF_COLS):
        cols = slice(j * FF_COLS, (j + 1) * FF_COLS)
        a = up[:, cols]
        g = up[:, D_FF + j * FF_COLS:D_FF + (j + 1) * FF_COLS]
        w0 = cw_ref[0:1, cols]
        w1 = cw_ref[1:2, cols]
        w2 = cw_ref[2:3, cols]
        bias = cb_ref[:, cols]
        for nb in range(nb_blk):
            rows = slice(nb * tt, (nb + 1) * tt)
            an = a[rows]
            c0 = prev[nb, 0:1, cols]
            c1 = prev[nb, 1:2, cols]
            am1 = jnp.where(row == 0, c1, pltpu.roll(an, 1, axis=0))
            am2 = jnp.where(row == 0, c0, jnp.where(row == 1, c1, pltpu.roll(an, 2, axis=0)))
            carry_ref[nb, :, cols] = an[tt - 2:tt]
            conv = bias + w0 * am2 + w1 * am1 + w2 * an
            hmid_ref[rows, cols] = (jax.nn.gelu(conv) * g[rows]).astype(BF16)

    out = x + _dot(hmid_ref[...], wdn_ref[...])
    if final_norm:
        out = _rms(out, gf_ref[...])
    y_ref[...] = out.reshape(nb_blk, tt, D_MODEL)

    @pl.when(t == pl.num_programs(1) - 1)
    def _():
        cst_ref[...] = carry_ref[...]


def _ffn_layer(x, att, w_o, s0, norm_g, w_up, conv_w, conv_b, w_down, norm_final, layer, nb_blk, tt):
    B, T, D = x.shape
    has_pre = att is not None
    has_s0 = s0 is not None
    final_norm = norm_final is not None
    grid = (B // nb_blk, T // tt)
    x_spec = pl.BlockSpec((nb_blk, tt, D), lambda b, t: (b, t, 0))
    cs_spec = pl.BlockSpec((nb_blk, CONV_WIDTH - 1, D_FF), lambda b, t: (b, 0, 0))
    in_specs = [x_spec]
    args = [x]
    if has_pre:
        in_specs += [x_spec, _const_spec(w_o.shape)]
        args += [att, w_o]
    if has_s0:
        in_specs.append(cs_spec)
        args.append(s0)
    in_specs += [_const_spec(norm_g.shape), _layer_spec(w_up.shape, layer), _const_spec(conv_w.shape),
                 _const_spec(conv_b.shape), _layer_spec(w_down.shape, layer)]
    args += [norm_g, w_up, conv_w, conv_b, w_down]
    if final_norm:
        in_specs.append(_const_spec(norm_final.shape))
        args.append(norm_final)
    return pl.pallas_call(
        functools.partial(_ffn_kernel, nb_blk=nb_blk, tt=tt, has_s0=has_s0, has_pre=has_pre,
                          final_norm=final_norm),
        grid=grid,
        in_specs=in_specs,
        out_specs=[x_spec, cs_spec],
        out_shape=[jax.ShapeDtypeStruct((B, T, D), F32),
                   jax.ShapeDtypeStruct((B, CONV_WIDTH - 1, D_FF), F32)],
        scratch_shapes=[pltpu.VMEM((nb_blk, CONV_WIDTH - 1, D_FF), F32),
                        pltpu.VMEM((nb_blk * tt, D_FF), BF16)],
        compiler_params=pltpu.CompilerParams(
            dimension_semantics=("arbitrary", "arbitrary"), vmem_limit_bytes=VMEM_LIMIT),
        name="conv_ffn",
    )(*args)


def _qkv_kernel(x_ref, g_ref, w_ref, q_ref, k_ref, v_ref, kb_ref, vb_ref, *, nb_blk, tt, feature_major):
    m = nb_blk * tt
    shp = (nb_blk, tt, D_MODEL)
    hn = _rms(x_ref[...].reshape(m, D_MODEL), g_ref[...]).astype(BF16)
    proj = _dot(hn, w_ref[...])
    q_ref[...] = (proj[:, :D_MODEL] * (SB_HEAD_DIM ** -0.5)).astype(BF16).reshape(shp)
    k = proj[:, D_MODEL:2 * D_MODEL]
    v = proj[:, 2 * D_MODEL:]
    if feature_major:
        for nb in range(nb_blk):
            k_ref[nb] = k[nb * tt:(nb + 1) * tt].T
            v_ref[nb] = v[nb * tt:(nb + 1) * tt].T
    else:
        k_ref[...] = k.reshape(shp)
        v_ref[...] = v.reshape(shp)
    kb_ref[...] = k.astype(BF16).reshape(shp)
    vb_ref[...] = v.astype(BF16).reshape(shp)


def _qkv_proj(x, norm_g, w_in, nb_blk, tt, feature_major):
    B, T, D = x.shape
    x_spec = pl.BlockSpec((nb_blk, tt, D), lambda b, t: (b, t, 0))
    kv_spec = pl.BlockSpec((nb_blk, D, tt), lambda b, t: (b, 0, t)) if feature_major else x_spec
    kv_shape = (B, D, T) if feature_major else (B, T, D)
    return pl.pallas_call(
        functools.partial(_qkv_kernel, nb_blk=nb_blk, tt=tt, feature_major=feature_major),
        grid=(B // nb_blk, T // tt),
        in_specs=[x_spec, _const_spec(norm_g.shape), _const_spec(w_in.shape)],
        out_specs=[x_spec, kv_spec, kv_spec, x_spec, x_spec],
        out_shape=[jax.ShapeDtypeStruct((B, T, D), BF16),
                   jax.ShapeDtypeStruct(kv_shape, F32),
                   jax.ShapeDtypeStruct(kv_shape, F32),
                   jax.ShapeDtypeStruct((B, T, D), BF16),
                   jax.ShapeDtypeStruct((B, T, D), BF16)],
        compiler_params=pltpu.CompilerParams(
            dimension_semantics=("arbitrary", "arbitrary"), vmem_limit_bytes=VMEM_LIMIT),
        name="sb_qkv",
    )(x, norm_g, w_in)


def _strict_lower(n):
    j = lax.broadcasted_iota(jnp.int32, (n, n), 0)
    s = lax.broadcasted_iota(jnp.int32, (n, n), 1)
    return jnp.where(j > s, 1.0, 0.0).astype(BF16)


def _split_pair(a):
    lane = lax.broadcasted_iota(jnp.int32, a.shape, 1)
    zero = jnp.zeros_like(a)
    return jnp.where(lane < SB_HEAD_DIM, a, zero), jnp.where(lane >= SB_HEAD_DIM, a, zero)


F32_EXP_ZERO = 104.0


def _any_live(runs):
    m = runs[0]
    for r in runs[1:]:
        m = jnp.minimum(m, r)
    return (jnp.min(m) < F32_EXP_ZERO).astype(jnp.int32)


def _sb_heads_blocks(qs, kss, vss, us, runs, masks, kv_t=False, stack_heads=False):
    zs = [[_dot(q, k) if kv_t else _dot_nt(q, k) for q, k in zip(qs, ks)] for ks in kss]
    sps = [[jnp.maximum(z, 0.0) + jnp.log(1.0 + jnp.exp(-jnp.abs(z))) for z in zb] for zb in zs]
    sps = [spb if m is None else [jnp.where(m, sp, 0.0) for sp in spb] for spb, m in zip(sps, masks)]
    if stack_heads:
        n = sps[0][0].shape[0]
        stacked = [_dot(jnp.concatenate(spb, axis=0).astype(BF16), u) for spb, u in zip(sps, us)]
        laters = [[st[h * n:(h + 1) * n] for h in range(len(qs))] for st in stacked]
    else:
        laters = [[_dot(sp.astype(BF16), u) for sp in spb] for spb, u in zip(sps, us)]
    avs = None
    for zb, spb, lb, vs, m in zip(zs, sps, laters, vss, masks):
        ws = [jnp.exp(((z - sp) - later) - run) for z, sp, later, run in zip(zb, spb, lb, runs)]
        if m is not None:
            ws = [jnp.where(m, w, 0.0) for w in ws]
        av = [_dot_nt(w.astype(BF16), v) if kv_t else _dot(w.astype(BF16), v) for w, v in zip(ws, vs)]
        avs = av if avs is None else [a + b for a, b in zip(avs, av)]
        runs = [run + later[:, 0:1] + sp[:, 0:1] for run, later, sp in zip(runs, lb, spb)]
    return avs, runs


def _attn_prompt_kernel(q_ref, k_ref, v_ref, o_ref, *, tq, n_lock):
    qi = pl.program_id(2)
    lanes = [slice(p * SB_PAIR, (p + 1) * SB_PAIR) for p in range(n_lock)]
    q_all = q_ref[0]
    qs = [qh for sl in lanes for qh in _split_pair(q_all[:, sl])]
    u = _strict_lower(tq)
    row = lax.broadcasted_iota(jnp.int32, (tq, tq), 0)
    col = lax.broadcasted_iota(jnp.int32, (tq, tq), 1)
    accs = tuple(jnp.zeros((tq, SB_PAIR), F32) for _ in lanes)
    runs = tuple(jnp.zeros((tq, 1), F32) for _ in qs)

    def visit(blocks, accs, runs, masks):
        kss, vss = [], []
        for j in blocks:
            k0 = pl.multiple_of(j * tq, tq)
            kb = k_ref[0, pl.ds(k0, tq), :]
            vb = v_ref[0, pl.ds(k0, tq), :]
            kss.append([kb[:, sl] for sl in lanes for _ in range(2)])
            vss.append([vh for sl in lanes for vh in _split_pair(vb[:, sl])])
        avs, runs = _sb_heads_blocks(qs, kss, vss, [u] * len(blocks), runs, masks)
        accs = tuple(acc + avs[2 * p] + avs[2 * p + 1] for p, acc in enumerate(accs))
        return accs, tuple(runs)

    def finish(j, accs, runs):
        def cond(c):
            j, alive, _, _ = c
            return (j >= 0) & (alive > 0)

        def body(c):
            j, _, accs, runs = c
            accs, runs = visit([j], accs, runs, [None])
            return j - 1, _any_live(runs), accs, runs

        _, _, accs, _ = lax.while_loop(cond, body, (j, _any_live(runs), accs, runs))
        o_ref[0] = jnp.concatenate(accs, axis=1).astype(BF16)

    @pl.when(qi == 0)
    def _():
        finish(qi - 1, *visit([qi], accs, runs, [col < row]))

    @pl.when(qi > 0)
    def _():
        finish(qi - 2, *visit([qi, qi - 1], accs, runs, [col < row, None]))


def _attn_prompt(q, kb, vb, tq, n_lock):
    B, T, D = q.shape
    width = n_lock * SB_PAIR
    q_spec = pl.BlockSpec((1, tq, width), lambda b, p, i: (b, i, p))
    kv_spec = pl.BlockSpec((1, T, width), lambda b, p, i: (b, 0, p))
    return pl.pallas_call(
        functools.partial(_attn_prompt_kernel, tq=tq, n_lock=n_lock),
        grid=(B, D // width, T // tq),
        in_specs=[q_spec, kv_spec, kv_spec],
        out_specs=q_spec,
        out_shape=jax.ShapeDtypeStruct((B, T, D), BF16),
        compiler_params=pltpu.CompilerParams(
            dimension_semantics=("arbitrary", "arbitrary", "arbitrary"), vmem_limit_bytes=VMEM_LIMIT),
        name="sb_attn_prompt",
    )(q, kb, vb)


def _attn_sample_kernel(q_ref, kn_ref, vn_ref, kc_hbm, vc_hbm, o_ref, kbuf, vbuf, acc_ref, run_ref, sem,
                        *, tq, tk, n_past):
    b = pl.program_id(0)
    j0 = n_past - 1
    own_slot = 2 + lax.rem(b, 2)

    def copies(stream, j, slot):
        k0 = pl.multiple_of(j * tk, tk)
        return (pltpu.make_async_copy(kc_hbm.at[stream, :, :, pl.ds(k0, tk)], kbuf.at[slot], sem.at[0, slot]),
                pltpu.make_async_copy(vc_hbm.at[stream, :, :, pl.ds(k0, tk)], vbuf.at[slot], sem.at[1, slot]))

    def start(stream, j, slot):
        for c in copies(stream, j, slot):
            c.start()

    def wait(j, slot):
        for c in copies(b, j, slot):
            c.wait()

    def slot_of(j):
        return jnp.where(j == j0, own_slot, lax.rem(j0 - j, 2))

    @pl.when(b == 0)
    def _():
        start(b, j0, own_slot)

    @pl.when(b + 1 < pl.num_programs(0))
    def _():
        start(b + 1, j0, 5 - own_slot)

    row = lax.broadcasted_iota(jnp.int32, (tq, tq), 0)
    col = lax.broadcasted_iota(jnp.int32, (tq, tq), 1)
    u_new = _strict_lower(tq)
    q = q_ref[0]
    kn = kn_ref[0]
    vn = vn_ref[0]
    head_lanes = [slice(h * SB_HEAD_DIM, (h + 1) * SB_HEAD_DIM) for h in range(SB_HEADS)]
    q_heads = [q[:, sl] for sl in head_lanes]
    avs, runs = _sb_heads_blocks(q_heads, [[kn[:, sl] for sl in head_lanes]],
                                 [[vn[:, sl] for sl in head_lanes]], [u_new],
                                 [jnp.zeros((tq, 1), F32)] * SB_HEADS, [col < row], stack_heads=True)
    for h in range(SB_HEADS):
        acc_ref[h] = avs[h]
        run_ref[h] = runs[h]
    u = _strict_lower(tk)

    def cond(c):
        j, live = c
        return (j >= 0) & (live > 0)

    def body(c):
        j, _ = c
        slot = slot_of(j)
        wait(j, slot)

        @pl.when(j > 0)
        def _():
            start(b, j - 1, slot_of(j - 1))

        ks = [kbuf[slot, h].astype(BF16) for h in range(SB_HEADS)]
        vs = [vbuf[slot, h].astype(BF16) for h in range(SB_HEADS)]
        avs, runs = _sb_heads_blocks(q_heads, [ks], [vs], [u], [run_ref[h] for h in range(SB_HEADS)],
                                     [None], kv_t=True, stack_heads=True)
        for h in range(SB_HEADS):
            acc_ref[h] += avs[h]
            run_ref[h] = runs[h]
        return j - 1, _any_live(runs)

    jf, _ = lax.while_loop(cond, body, (j0, _any_live(runs)))

    @pl.when(jf >= 0)
    def _():
        wait(jf, slot_of(jf))

    o_ref[0] = jnp.concatenate([acc_ref[h] for h in range(SB_HEADS)], axis=1).astype(BF16)


def _attn_sample(q, kb, vb, k_past, v_past, tk):
    B, T, D = q.shape
    n_past = k_past.shape[3] // tk
    q_spec = pl.BlockSpec((1, T, D), lambda b: (b, 0, 0))
    any_spec = pl.BlockSpec(memory_space=pl.ANY)
    return pl.pallas_call(
        functools.partial(_attn_sample_kernel, tq=T, tk=tk, n_past=n_past),
        grid=(B,),
        in_specs=[q_spec, q_spec, q_spec, any_spec, any_spec],
        out_specs=q_spec,
        out_shape=jax.ShapeDtypeStruct((B, T, D), BF16),
        scratch_shapes=[pltpu.VMEM((4, SB_HEADS, SB_HEAD_DIM, tk), F32),
                        pltpu.VMEM((4, SB_HEADS, SB_HEAD_DIM, tk), F32),
                        pltpu.VMEM((SB_HEADS, T, SB_HEAD_DIM), F32),
                        pltpu.VMEM((SB_HEADS, T, 1), F32),
                        pltpu.SemaphoreType.DMA((2, 4))],
        compiler_params=pltpu.CompilerParams(
            dimension_semantics=("arbitrary",), vmem_limit_bytes=VMEM_LIMIT),
        name="sb_attn_sample",
    )(q, kb, vb, k_past, v_past)


def _step_tile(B, T):
    tt = min(T, ROWS_PER_STEP)
    return max(1, min(B, ROWS_PER_STEP // tt)), tt


def _trunk(x, hg_s0, k_past, v_past, conv_s0, p):
    B, T, D = x.shape
    nb_blk, tt = _step_tile(B, T)
    x1, hg_fin = _hgrn_layer(x, hg_s0, p["norm_mix"][0], p["w_in_a"], p["lb_logits"], p["gn_a"],
                             p["w_out_a"], 0, min(nb_blk, HG_STREAMS_PER_STEP), tt)
    x2, conv0 = _ffn_layer(x1, None, None, None if conv_s0 is None else conv_s0[0],
                           p["norm_ffn"][0], p["w_up"], p["conv_w"][0], p["conv_b"][0],
                           p["w_down"], None, 0, nb_blk, tt)
    feature_major = k_past is None
    q, k, v, kb, vb = _qkv_proj(x2, p["norm_mix"][1], p["w_in_b"], nb_blk, tt, feature_major)
    if k_past is None:
        att = _attn_prompt(q, kb, vb, SB_TILE, SB_LOCKSTEP_PAIRS)
    else:
        att = _attn_sample(q, kb, vb, k_past, v_past, SB_TILE)
    y, conv1 = _ffn_layer(x2, att, p["w_out_b"], None if conv_s0 is None else conv_s0[1],
                          p["norm_ffn"][1], p["w_up"], p["conv_w"][1], p["conv_b"][1],
                          p["w_down"], p["norm_final"], 1, nb_blk, tt)
    if feature_major:
        to_rows = lambda a: jnp.transpose(a.reshape(B, SB_HEADS, SB_HEAD_DIM, T), (0, 3, 1, 2))[None]
    else:
        to_rows = lambda a: a.reshape(B, T, SB_HEADS, SB_HEAD_DIM)[None]
    return (y, hg_fin[None], to_rows(k), to_rows(v), jnp.stack([conv0, conv1]))


def kernel(x_prompt, x_sample, state_hgrn, cache_k, cache_v, state_conv, norm_mix, norm_ffn, norm_final,
           w_in_a, w_out_a, gn_a, lb_logits, w_in_b, w_out_b, w_up, conv_w, conv_b, w_down):
    row = lambda a: a.reshape(a.shape[:-1] + (1, a.shape[-1]))
    p = {
        "norm_mix": row(norm_mix), "norm_ffn": row(norm_ffn), "norm_final": row(norm_final),
        "w_in_a": w_in_a[0].astype(BF16), "w_out_a": w_out_a[0].astype(BF16),
        "gn_a": gn_a[0].reshape(1, HG_HEADS * HG_DK), "lb_logits": lb_logits,
        "w_in_b": w_in_b[0].astype(BF16), "w_out_b": w_out_b[0].astype(BF16),
        "w_up": w_up.astype(BF16), "conv_w": conv_w, "conv_b": row(conv_b),
        "w_down": w_down.astype(BF16),
    }
    yp, hgp, kp, vp, cp = _trunk(x_prompt, None, None, None, None, p)
    ys, hgs, ks, vs, cs = _trunk(
        x_sample, state_hgrn[0], jnp.transpose(cache_k[0], (0, 2, 3, 1)),
        jnp.transpose(cache_v[0], (0, 2, 3, 1)), state_conv, p)
    return (yp, ys, hgp, kp, vp, cp, hgs, ks, vs, cs)
```

```python
import functools

import jax
import jax.numpy as jnp
from jax import lax
from jax.experimental import pallas as pl
from jax.experimental.pallas import tpu as pltpu

F32 = jnp.float32
BF16 = jnp.bfloat16

D_MODEL = 1024
CHUNK = 64
HG_HEADS = 8
HG_DK = 128
HG_UNIT_ROWS = 256
HG_STREAMS_PER_STEP = 4
SB_HEADS = 16
SB_HEAD_DIM = 64
SB_PAIR = 2 * SB_HEAD_DIM
D_FF = 2816
FF_COLS = 256
CONV_WIDTH = 3
EPS = 1e-6
VMEM_LIMIT = 56 * 1024 * 1024
ROWS_PER_STEP = 512
SB_TILE = 256
SB_LOCKSTEP_PAIRS = 4

_NT = (((1,), (1,)), ((), ()))


def _dot(a, b):
    return jnp.dot(a, b, preferred_element_type=F32)


def _dot_nt(a, b):
    return lax.dot_general(a, b, _NT, preferred_element_type=F32)


def _rms(x, g):
    r = lax.rsqrt(jnp.mean(x * x, axis=-1, keepdims=True) + EPS)
    return x * r * g


def _layer_spec(shape, layer):
    nd = len(shape) - 1
    return pl.BlockSpec((None,) + tuple(shape[1:]), lambda *_: (layer,) + (0,) * nd,
                        pipeline_mode=pl.Buffered(1))


def _const_spec(shape):
    nd = len(shape)
    return pl.BlockSpec(shape, lambda *_: (0,) * nd, pipeline_mode=pl.Buffered(1))


def _hgrn_kernel(*refs, nb_blk, tt, has_s0, n_lb_rows):
    if has_s0:
        (x_ref, s0_ref, g_ref, win_ref, lbl_ref, gn_ref, wout_ref,
         o_ref, sfin_ref, st_ref) = refs
    else:
        (x_ref, g_ref, win_ref, lbl_ref, gn_ref, wout_ref,
         o_ref, sfin_ref, st_ref) = refs
        s0_ref = None
    t = pl.program_id(1)
    m = nb_blk * tt
    ur = min(m, HG_UNIT_ROWS)
    kd = HG_HEADS * HG_DK

    @pl.when(t == 0)
    def _():
        if has_s0:
            for nb in range(nb_blk):
                for h in range(HG_HEADS):
                    st_ref[nb, h] = s0_ref[nb, h].T
        else:
            st_ref[...] = jnp.zeros_like(st_ref)

    ll = lbl_ref[...]
    e = jnp.exp(ll - jnp.max(ll, axis=0, keepdims=True))
    sm = e / jnp.sum(e, axis=0, keepdims=True)
    lb = jnp.sum(sm[:n_lb_rows], axis=0, keepdims=True)

    ri = lax.broadcasted_iota(jnp.int32, (ur, ur), 0)
    ci = lax.broadcasted_iota(jnp.int32, (ur, ur), 1)
    tri = jnp.where(((ri >> 6) == (ci >> 6)) & (ci <= ri), 1.0, 0.0).astype(BF16)
    gn = gn_ref[...]
    t64 = lax.broadcasted_iota(jnp.int32, (CHUNK, CHUNK), 0)
    s64 = lax.broadcasted_iota(jnp.int32, (CHUNK, CHUNK), 1)
    causal = s64 <= t64
    head_lanes = [slice(h * HG_DK, (h + 1) * HG_DK) for h in range(HG_HEADS)]
    x_all = x_ref[...].reshape(m, D_MODEL)

    def project(u0):
        x = x_all[u0:u0 + ur]
        hn = _rms(x, g_ref[...]).astype(BF16)
        proj = _dot(hn, win_ref[...])
        q = proj[:, :kd]
        fpre = proj[:, kd:2 * kd]
        v = proj[:, 2 * kd:3 * kd]
        gate = proj[:, 3 * kd:]
        en = jnp.exp(-jnp.abs(fpre))
        r = 1.0 / (1.0 + en)
        er = en * r
        pos = fpre >= 0.0
        logf = jnp.log(lb + (1.0 - lb) * jnp.where(pos, r, er))
        k = (1.0 - lb) * jnp.where(pos, er, r)
        hi = logf.astype(BF16)
        lo = (logf - hi.astype(F32)).astype(BF16)
        b = _dot(tri, hi) + _dot(tri, lo)
        return x, q, k, v, b, jax.nn.sigmoid(gate)

    units = list(range(0, m, ur))
    projected = [project(u0) for u0 in units]
    sts = {}
    for u0, (x, q, k, v, b, sgate) in zip(units, projected):
        ogs = []
        for c in range(ur // CHUNK):
            nb = (u0 + c * CHUNK) // tt
            if nb not in sts:
                sts[nb] = [st_ref[nb, h] for h in range(HG_HEADS)]
            rows = slice(c * CHUNK, (c + 1) * CHUNK)
            bc = b[rows]
            bl = bc[CHUNK - 1:CHUNK]
            qt = (q[rows] * jnp.exp(bc)).astype(BF16)
            kt = (k[rows] * jnp.exp(-bc)).astype(BF16)
            ke = (k[rows] * jnp.exp(bl - bc)).astype(BF16)
            ebl = jnp.exp(bl)
            vc = v[rows]
            attns = [jnp.where(causal, _dot_nt(qt[:, sl], kt[:, sl]), 0.0) for sl in head_lanes]
            inters = [_dot_nt(qt[:, sl], st.astype(BF16)) for sl, st in zip(head_lanes, sts[nb])]
            upds = [_dot(vc[:, sl].T.astype(BF16), ke[:, sl]) for sl in head_lanes]
            outs = [_dot(attn.astype(BF16), vc[:, sl].astype(BF16)) + inter
                    for attn, sl, inter in zip(attns, head_lanes, inters)]
            sts[nb] = [st * ebl[:, sl] + upd for st, sl, upd in zip(sts[nb], head_lanes, upds)]
            ons = [o * lax.rsqrt(jnp.mean(o * o, axis=-1, keepdims=True) + EPS) for o in outs]
            ogs.append((jnp.concatenate(ons, axis=1) * gn * sgate[rows]).astype(BF16))
        out = x + _dot(jnp.concatenate(ogs, axis=0), wout_ref[...])
        if tt >= ur:
            o_ref[u0 // tt, u0 % tt:u0 % tt + ur, :] = out
        else:
            o_ref[u0 // tt:(u0 + ur) // tt] = out.reshape(ur // tt, tt, D_MODEL)
    for nb, st in sts.items():
        for h in range(HG_HEADS):
            st_ref[nb, h] = st[h]

    @pl.when(t == pl.num_programs(1) - 1)
    def _():
        for nb in range(nb_blk):
            for h in range(HG_HEADS):
                sfin_ref[nb, h] = st_ref[nb, h].T


def _hgrn_layer(x, s0, norm_g, w_in, lb_logits, gn, w_out, layer_idx, nb_blk, tt):
    B, T, D = x.shape
    has_s0 = s0 is not None
    grid = (B // nb_blk, T // tt)
    x_spec = pl.BlockSpec((nb_blk, tt, D), lambda b, t: (b, t, 0))
    st_spec = pl.BlockSpec((nb_blk, HG_HEADS, HG_DK, HG_DK), lambda b, t: (b, 0, 0, 0))
    in_specs = [x_spec]
    args = [x]
    if has_s0:
        in_specs.append(st_spec)
        args.append(s0)
    in_specs += [_const_spec(norm_g.shape), _const_spec(w_in.shape), _const_spec(lb_logits.shape),
                 _const_spec(gn.shape), _const_spec(w_out.shape)]
    args += [norm_g, w_in, lb_logits, gn, w_out]
    return pl.pallas_call(
        functools.partial(_hgrn_kernel, nb_blk=nb_blk, tt=tt, has_s0=has_s0, n_lb_rows=layer_idx + 1),
        grid=grid,
        in_specs=in_specs,
        out_specs=[x_spec, st_spec],
        out_shape=[jax.ShapeDtypeStruct((B, T, D), F32),
                   jax.ShapeDtypeStruct((B, HG_HEADS, HG_DK, HG_DK), F32)],
        scratch_shapes=[pltpu.VMEM((nb_blk, HG_HEADS, HG_DK, HG_DK), F32)],
        compiler_params=pltpu.CompilerParams(
            dimension_semantics=("arbitrary", "arbitrary"), vmem_limit_bytes=VMEM_LIMIT),
        name="hgrn_layer",
    )(*args)


def _ffn_kernel(*refs, nb_blk, tt, has_s0, has_pre, final_norm):
    refs = list(refs)
    x_ref = refs.pop(0)
    att_ref = refs.pop(0) if has_pre else None
    wo_ref = refs.pop(0) if has_pre else None
    s0_ref = refs.pop(0) if has_s0 else None
    g_ref, wup_ref, cw_ref, cb_ref, wdn_ref = refs[:5]
    refs = refs[5:]
    gf_ref = refs.pop(0) if final_norm else None
    y_ref, cst_ref, carry_ref, hmid_ref = refs
    t = pl.program_id(1)
    m = nb_blk * tt

    @pl.when(t == 0)
    def _():
        if has_s0:
            carry_ref[...] = s0_ref[...]
        else:
            carry_ref[...] = jnp.zeros_like(carry_ref)

    x = x_ref[...].reshape(m, D_MODEL)
    if has_pre:
        x = x + _dot(att_ref[...].reshape(m, D_MODEL), wo_ref[...])
    hn = _rms(x, g_ref[...]).astype(BF16)
    up = _dot(hn, wup_ref[...])
    row = lax.broadcasted_iota(jnp.int32, (tt, FF_COLS), 0)
    prev = carry_ref[...]

    for j in range(D_FF // FF_COLS):
        cols = slice(j * FF_COLS, (j + 1) * FF_COLS)
        a = up[:, cols]
        g = up[:, D_FF + j * FF_COLS:D_FF + (j + 1) * FF_COLS]
        w0 = cw_ref[0:1, cols]
        w1 = cw_ref[1:2, cols]
        w2 = cw_ref[2:3, cols]
        bias = cb_ref[:, cols]
        for nb in range(nb_blk):
            rows = slice(nb * tt, (nb + 1) * tt)
            an = a[rows]
            c0 = prev[nb, 0:1, cols]
            c1 = prev[nb, 1:2, cols]
            am1 = jnp.where(row == 0, c1, pltpu.roll(an, 1, axis=0))
            am2 = jnp.where(row == 0, c0, jnp.where(row == 1, c1, pltpu.roll(an, 2, axis=0)))
            carry_ref[nb, :, cols] = an[tt - 2:tt]
            conv = bias + w0 * am2 + w1 * am1 + w2 * an
            hmid_ref[rows, cols] = (jax.nn.gelu(conv) * g[rows]).astype(BF16)

    out = x + _dot(hmid_ref[...], wdn_ref[...])
    if final_norm:
        out = _rms(out, gf_ref[...])
    y_ref[...] = out.reshape(nb_blk, tt, D_MODEL)

    @pl.when(t == pl.num_programs(1) - 1)
    def _():
        cst_ref[...] = carry_ref[...]


def _ffn_layer(x, att, w_o, s0, norm_g, w_up, conv_w, conv_b, w_down, norm_final, layer, nb_blk, tt):
    B, T, D = x.shape
    has_pre = att is not None
    has_s0 = s0 is not None
    final_norm = norm_final is not None
    grid = (B // nb_blk, T // tt)
    x_spec = pl.BlockSpec((nb_blk, tt, D), lambda b, t: (b, t, 0))
    cs_spec = pl.BlockSpec((nb_blk, CONV_WIDTH - 1, D_FF), lambda b, t: (b, 0, 0))
    in_specs = [x_spec]
    args = [x]
    if has_pre:
        in_specs += [x_spec, _const_spec(w_o.shape)]
        args += [att, w_o]
    if has_s0:
        in_specs.append(cs_spec)
        args.append(s0)
    in_specs += [_const_spec(norm_g.shape), _layer_spec(w_up.shape, layer), _const_spec(conv_w.shape),
                 _const_spec(conv_b.shape), _layer_spec(w_down.shape, layer)]
    args += [norm_g, w_up, conv_w, conv_b, w_down]
    if final_norm:
        in_specs.append(_const_spec(norm_final.shape))
        args.append(norm_final)
    return pl.pallas_call(
        functools.partial(_ffn_kernel, nb_blk=nb_blk, tt=tt, has_s0=has_s0, has_pre=has_pre,
                          final_norm=final_norm),
        grid=grid,
        in_specs=in_specs,
        out_specs=[x_spec, cs_spec],
        out_shape=[jax.ShapeDtypeStruct((B, T, D), F32),
                   jax.ShapeDtypeStruct((B, CONV_WIDTH - 1, D_FF), F32)],
        scratch_shapes=[pltpu.VMEM((nb_blk, CONV_WIDTH - 1, D_FF), F32),
                        pltpu.VMEM((nb_blk * tt, D_FF), BF16)],
        compiler_params=pltpu.CompilerParams(
            dimension_semantics=("arbitrary", "arbitrary"), vmem_limit_bytes=VMEM_LIMIT),
        name="conv_ffn",
    )(*args)


def _qkv_kernel(x_ref, g_ref, w_ref, q_ref, k_ref, v_ref, kb_ref, vb_ref, *, nb_blk, tt, feature_major):
    m = nb_blk * tt
    shp = (nb_blk, tt, D_MODEL)
    hn = _rms(x_ref[...].reshape(m, D_MODEL), g_ref[...]).astype(BF16)
    proj = _dot(hn, w_ref[...])
    q_ref[...] = (proj[:, :D_MODEL] * (SB_HEAD_DIM ** -0.5)).astype(BF16).reshape(shp)
    k = proj[:, D_MODEL:2 * D_MODEL]
    v = proj[:, 2 * D_MODEL:]
    if feature_major:
        for nb in range(nb_blk):
            k_ref[nb] = k[nb * tt:(nb + 1) * tt].T
            v_ref[nb] = v[nb * tt:(nb + 1) * tt].T
    else:
        k_ref[...] = k.reshape(shp)
        v_ref[...] = v.reshape(shp)
    kb_ref[...] = k.astype(BF16).reshape(shp)
    vb_ref[...] = v.astype(BF16).reshape(shp)


def _qkv_proj(x, norm_g, w_in, nb_blk, tt, feature_major):
    B, T, D = x.shape
    x_spec = pl.BlockSpec((nb_blk, tt, D), lambda b, t: (b, t, 0))
    kv_spec = pl.BlockSpec((nb_blk, D, tt), lambda b, t: (b, 0, t)) if feature_major else x_spec
    kv_shape = (B, D, T) if feature_major else (B, T, D)
    return pl.pallas_call(
        functools.partial(_qkv_kernel, nb_blk=nb_blk, tt=tt, feature_major=feature_major),
        grid=(B // nb_blk, T // tt),
        in_specs=[x_spec, _const_spec(norm_g.shape), _const_spec(w_in.shape)],
        out_specs=[x_spec, kv_spec, kv_spec, x_spec, x_spec],
        out_shape=[jax.ShapeDtypeStruct((B, T, D), BF16),
                   jax.ShapeDtypeStruct(kv_shape, F32),
                   jax.ShapeDtypeStruct(kv_shape, F32),
                   jax.ShapeDtypeStruct((B, T, D), BF16),
                   jax.ShapeDtypeStruct((B, T, D), BF16)],
        compiler_params=pltpu.CompilerParams(
            dimension_semantics=("arbitrary", "arbitrary"), vmem_limit_bytes=VMEM_LIMIT),
        name="sb_qkv",
    )(x, norm_g, w_in)


def _strict_lower(n):
    j = lax.broadcasted_iota(jnp.int32, (n, n), 0)
    s = lax.broadcasted_iota(jnp.int32, (n, n), 1)
    return jnp.where(j > s, 1.0, 0.0).astype(BF16)


def _split_pair(a):
    lane = lax.broadcasted_iota(jnp.int32, a.shape, 1)
    zero = jnp.zeros_like(a)
    return jnp.where(lane < SB_HEAD_DIM, a, zero), jnp.where(lane >= SB_HEAD_DIM, a, zero)


F32_EXP_ZERO = 104.0


def _any_live(runs):
    m = runs[0]
    for r in runs[1:]:
        m = jnp.minimum(m, r)
    return (jnp.min(m) < F32_EXP_ZERO).astype(jnp.int32)


def _sb_heads_blocks(qs, kss, vss, us, runs, masks, kv_t=False, stack_heads=False):
    zs = [[_dot(q, k) if kv_t else _dot_nt(q, k) for q, k in zip(qs, ks)] for ks in kss]
    sps = [[jnp.maximum(z, 0.0) + jnp.log(1.0 + jnp.exp(-jnp.abs(z))) for z in zb] for zb in zs]
    sps = [spb if m is None else [jnp.where(m, sp, 0.0) for sp in spb] for spb, m in zip(sps, masks)]
    if stack_heads:
        n = sps[0][0].shape[0]
        stacked = [_dot(jnp.concatenate(spb, axis=0).astype(BF16), u) for spb, u in zip(sps, us)]
        laters = [[st[h * n:(h + 1) * n] for h in range(len(qs))] for st in stacked]
    else:
        laters = [[_dot(sp.astype(BF16), u) for sp in spb] for spb, u in zip(sps, us)]
    avs = None
    for zb, spb, lb, vs, m in zip(zs, sps, laters, vss, masks):
        ws = [jnp.exp(((z - sp) - later) - run) for z, sp, later, run in zip(zb, spb, lb, runs)]
        if m is not None:
            ws = [jnp.where(m, w, 0.0) for w in ws]
        av = [_dot_nt(w.astype(BF16), v) if kv_t else _dot(w.astype(BF16), v) for w, v in zip(ws, vs)]
        avs = av if avs is None else [a + b for a, b in zip(avs, av)]
        runs = [run + later[:, 0:1] + sp[:, 0:1] for run, later, sp in zip(runs, lb, spb)]
    return avs, runs


def _attn_prompt_kernel(q_ref, k_ref, v_ref, o_ref, *, tq, n_lock):
    i = pl.program_id(2)
    lanes = [slice(p * SB_PAIR, (p + 1) * SB_PAIR) for p in range(n_lock)]
    q_all = q_ref[0]
    tiles = (slice(0, tq), slice(tq, 2 * tq))
    q_lo, q_hi = ([qh for sl in lanes for qh in _split_pair(q_all[rows, sl])] for rows in tiles)
    u = _strict_lower(tq)
    row = lax.broadcasted_iota(jnp.int32, (tq, tq), 0)
    col = lax.broadcasted_iota(jnp.int32, (tq, tq), 1)
    causal = col < row
    zero_acc = tuple(jnp.zeros((tq, SB_PAIR), F32) for _ in lanes)
    zero_run = tuple(jnp.zeros((tq, 1), F32) for _ in q_lo)

    def kv(j):
        k0 = pl.multiple_of(j * tq, tq)
        kb = k_ref[0, pl.ds(k0, tq), :]
        vb = v_ref[0, pl.ds(k0, tq), :]
        return ([kb[:, sl] for sl in lanes for _ in range(2)],
                [vh for sl in lanes for vh in _split_pair(vb[:, sl])])

    def visit(qs, blocks, accs, runs, masks):
        avs, runs = _sb_heads_blocks(qs, [b[0] for b in blocks], [b[1] for b in blocks],
                                     [u] * len(blocks), runs, masks)
        accs = tuple(acc + avs[2 * p] + avs[2 * p + 1] for p, acc in enumerate(accs))
        return accs, tuple(runs)

    def finish(qs, rows, j, accs, runs):
        def cond(c):
            j, alive, _, _ = c
            return (j >= 0) & (alive > 0)

        def body(c):
            j, _, accs, runs = c
            accs, runs = visit(qs, [kv(j)], accs, runs, [None])
            return j - 1, _any_live(runs), accs, runs

        _, _, accs, _ = lax.while_loop(cond, body, (j, _any_live(runs), accs, runs))
        o_ref[0, rows, :] = jnp.concatenate(accs, axis=1).astype(BF16)

    @pl.when(i == 0)
    def _():
        own = kv(0)
        lo = visit(q_lo, [own], zero_acc, zero_run, [causal])
        hi = visit(q_hi, [kv(1), own], zero_acc, zero_run, [causal, None])
        for rows, (accs, _) in zip(tiles, (lo, hi)):
            o_ref[0, rows, :] = jnp.concatenate(accs, axis=1).astype(BF16)

    @pl.when(i > 0)
    def _():
        older, mid, newer = kv(2 * i - 1), kv(2 * i), kv(2 * i + 1)
        both = [(hi_b[0] + lo_b[0], hi_b[1] + lo_b[1]) for hi_b, lo_b in ((newer, mid), (mid, older))]
        accs, runs = visit(q_hi + q_lo, both, zero_acc + zero_acc, zero_run + zero_run, [causal, None])
        finish(q_hi, tiles[1], 2 * i - 1, accs[:n_lock], runs[:2 * n_lock])
        finish(q_lo, tiles[0], 2 * i - 2, accs[n_lock:], runs[2 * n_lock:])


def _attn_prompt(q, kb, vb, tq, n_lock):
    B, T, D = q.shape
    width = n_lock * SB_PAIR
    q_spec = pl.BlockSpec((1, 2 * tq, width), lambda b, p, i: (b, i, p))
    kv_spec = pl.BlockSpec((1, T, width), lambda b, p, i: (b, 0, p))
    return pl.pallas_call(
        functools.partial(_attn_prompt_kernel, tq=tq, n_lock=n_lock),
        grid=(B, D // width, T // (2 * tq)),
        in_specs=[q_spec, kv_spec, kv_spec],
        out_specs=q_spec,
        out_shape=jax.ShapeDtypeStruct((B, T, D), BF16),
        compiler_params=pltpu.CompilerParams(
            dimension_semantics=("arbitrary", "arbitrary", "arbitrary"), vmem_limit_bytes=VMEM_LIMIT),
        name="sb_attn_prompt",
    )(q, kb, vb)


def _attn_sample_kernel(q_ref, kn_ref, vn_ref, kc_hbm, vc_hbm, o_ref, kbuf, vbuf, acc_ref, run_ref, sem,
                        *, tq, tk, n_past):
    b = pl.program_id(0)
    j0 = n_past - 1
    own_slot = 2 + lax.rem(b, 2)

    def copies(stream, j, slot):
        k0 = pl.multiple_of(j * tk, tk)
        return (pltpu.make_async_copy(kc_hbm.at[stream, :, :, pl.ds(k0, tk)], kbuf.at[slot], sem.at[0, slot]),
                pltpu.make_async_copy(vc_hbm.at[stream, :, :, pl.ds(k0, tk)], vbuf.at[slot], sem.at[1, slot]))

    def start(stream, j, slot):
        for c in copies(stream, j, slot):
            c.start()

    def wait(j, slot):
        for c in copies(b, j, slot):
            c.wait()

    def slot_of(j):
        return jnp.where(j == j0, own_slot, lax.rem(j0 - j, 2))

    @pl.when(b == 0)
    def _():
        start(b, j0, own_slot)

    @pl.when(b + 1 < pl.num_programs(0))
    def _():
        start(b + 1, j0, 5 - own_slot)

    row = lax.broadcasted_iota(jnp.int32, (tq, tq), 0)
    col = lax.broadcasted_iota(jnp.int32, (tq, tq), 1)
    u_new = _strict_lower(tq)
    q = q_ref[0]
    kn = kn_ref[0]
    vn = vn_ref[0]
    head_lanes = [slice(h * SB_HEAD_DIM, (h + 1) * SB_HEAD_DIM) for h in range(SB_HEADS)]
    q_heads = [q[:, sl] for sl in head_lanes]
    avs, runs = _sb_heads_blocks(q_heads, [[kn[:, sl] for sl in head_lanes]],
                                 [[vn[:, sl] for sl in head_lanes]], [u_new],
                                 [jnp.zeros((tq, 1), F32)] * SB_HEADS, [col < row], stack_heads=True)
    for h in range(SB_HEADS):
        acc_ref[h] = avs[h]
        run_ref[h] = runs[h]
    u = _strict_lower(tk)

    def cond(c):
        j, live = c
        return (j >= 0) & (live > 0)

    def body(c):
        j, _ = c
        slot = slot_of(j)
        wait(j, slot)

        @pl.when(j > 0)
        def _():
            start(b, j - 1, slot_of(j - 1))

        ks = [kbuf[slot, h].astype(BF16) for h in range(SB_HEADS)]
        vs = [vbuf[slot, h].astype(BF16) for h in range(SB_HEADS)]
        avs, runs = _sb_heads_blocks(q_heads, [ks], [vs], [u], [run_ref[h] for h in range(SB_HEADS)],
                                     [None], kv_t=True, stack_heads=True)
        for h in range(SB_HEADS):
            acc_ref[h] += avs[h]
            run_ref[h] = runs[h]
        return j - 1, _any_live(runs)

    jf, _ = lax.while_loop(cond, body, (j0, _any_live(runs)))

    @pl.when(jf >= 0)
    def _():
        wait(jf, slot_of(jf))

    o_ref[0] = jnp.concatenate([acc_ref[h] for h in range(SB_HEADS)], axis=1).astype(BF16)


def _attn_sample(q, kb, vb, k_past, v_past, tk):
    B, T, D = q.shape
    n_past = k_past.shape[3] // tk
    q_spec = pl.BlockSpec((1, T, D), lambda b: (b, 0, 0))
    any_spec = pl.BlockSpec(memory_space=pl.ANY)
    return pl.pallas_call(
        functools.partial(_attn_sample_kernel, tq=T, tk=tk, n_past=n_past),
        grid=(B,),
        in_specs=[q_spec, q_spec, q_spec, any_spec, any_spec],
        out_specs=q_spec,
        out_shape=jax.ShapeDtypeStruct((B, T, D), BF16),
        scratch_shapes=[pltpu.VMEM((4, SB_HEADS, SB_HEAD_DIM, tk), F32),
                        pltpu.VMEM((4, SB_HEADS, SB_HEAD_DIM, tk), F32),
                        pltpu.VMEM((SB_HEADS, T, SB_HEAD_DIM), F32),
                        pltpu.VMEM((SB_HEADS, T, 1), F32),
                        pltpu.SemaphoreType.DMA((2, 4))],
        compiler_params=pltpu.CompilerParams(
            dimension_semantics=("arbitrary",), vmem_limit_bytes=VMEM_LIMIT),
        name="sb_attn_sample",
    )(q, kb, vb, k_past, v_past)


def _step_tile(B, T):
    tt = min(T, ROWS_PER_STEP)
    return max(1, min(B, ROWS_PER_STEP // tt)), tt


def _trunk(x, hg_s0, k_past, v_past, conv_s0, p):
    B, T, D = x.shape
    nb_blk, tt = _step_tile(B, T)
    x1, hg_fin = _hgrn_layer(x, hg_s0, p["norm_mix"][0], p["w_in_a"], p["lb_logits"], p["gn_a"],
                             p["w_out_a"], 0, min(nb_blk, HG_STREAMS_PER_STEP), tt)
    x2, conv0 = _ffn_layer(x1, None, None, None if conv_s0 is None else conv_s0[0],
                           p["norm_ffn"][0], p["w_up"], p["conv_w"][0], p["conv_b"][0],
                           p["w_down"], None, 0, nb_blk, tt)
    feature_major = k_past is None
    q, k, v, kb, vb = _qkv_proj(x2, p["norm_mix"][1], p["w_in_b"], nb_blk, tt, feature_major)
    if k_past is None:
        att = _attn_prompt(q, kb, vb, SB_TILE, SB_LOCKSTEP_PAIRS)
    else:
        att = _attn_sample(q, kb, vb, k_past, v_past, SB_TILE)
    y, conv1 = _ffn_layer(x2, att, p["w_out_b"], None if conv_s0 is None else conv_s0[1],
                          p["norm_ffn"][1], p["w_up"], p["conv_w"][1], p["conv_b"][1],
                          p["w_down"], p["norm_final"], 1, nb_blk, tt)
    if feature_major:
        to_rows = lambda a: jnp.transpose(a.reshape(B, SB_HEADS, SB_HEAD_DIM, T), (0, 3, 1, 2))[None]
    else:
        to_rows = lambda a: a.reshape(B, T, SB_HEADS, SB_HEAD_DIM)[None]
    return (y, hg_fin[None], to_rows(k), to_rows(v), jnp.stack([conv0, conv1]))


def kernel(x_prompt, x_sample, state_hgrn, cache_k, cache_v, state_conv, norm_mix, norm_ffn, norm_final,
           w_in_a, w_out_a, gn_a, lb_logits, w_in_b, w_out_b, w_up, conv_w, conv_b, w_down):
    row = lambda a: a.reshape(a.shape[:-1] + (1, a.shape[-1]))
    p = {
        "norm_mix": row(norm_mix), "norm_ffn": row(norm_ffn), "norm_final": row(norm_final),
        "w_in_a": w_in_a[0].astype(BF16), "w_out_a": w_out_a[0].astype(BF16),
        "gn_a": gn_a[0].reshape(1, HG_HEADS * HG_DK), "lb_logits": lb_logits,
        "w_in_b": w_in_b[0].astype(BF16), "w_out_b": w_out_b[0].astype(BF16),
        "w_up": w_up.astype(BF16), "conv_w": conv_w, "conv_b": row(conv_b),
        "w_down": w_down.astype(BF16),
    }
    yp, hgp, kp, vp, cp = _trunk(x_prompt, None, None, None, None, p)
    ys, hgs, ks, vs, cs = _trunk(
        x_sample, state_hgrn[0], jnp.transpose(cache_k[0], (0, 2, 3, 1)),
        jnp.transpose(cache_v[0], (0, 2, 3, 1)), state_conv, p)
    return (yp, ys, hgp, kp, vp, cp, hgs, ks, vs, cs)
```

```python
import functools

import jax
import jax.numpy as jnp
from jax import lax
from jax.experimental import pallas as pl
from jax.experimental.pallas import tpu as pltpu

F32 = jnp.float32
BF16 = jnp.bfloat16

D_MODEL = 1024
CHUNK = 64
HG_HEADS = 8
HG_DK = 128
HG_UNIT_ROWS = 256
HG_STREAMS_PER_STEP = 4
SB_HEADS = 16
SB_HEAD_DIM = 64
SB_PAIR = 2 * SB_HEAD_DIM
D_FF = 2816
FF_COLS = 256
CONV_WIDTH = 3
EPS = 1e-6
VMEM_LIMIT = 56 * 1024 * 1024
ROWS_PER_STEP = 512
SB_TILE = 256
SB_LOCKSTEP_PAIRS = 4

_NT = (((1,), (1,)), ((), ()))


def _dot(a, b):
    return jnp.dot(a, b, preferred_element_type=F32)


def _dot_nt(a, b):
    return lax.dot_general(a, b, _NT, preferred_element_type=F32)


def _rms(x, g):
    r = lax.rsqrt(jnp.mean(x * x, axis=-1, keepdims=True) + EPS)
    return x * r * g


def _layer_spec(shape, layer):
    nd = len(shape) - 1
    return pl.BlockSpec((None,) + tuple(shape[1:]), lambda *_: (layer,) + (0,) * nd,
                        pipeline_mode=pl.Buffered(1))


def _const_spec(shape):
    nd = len(shape)
    return pl.BlockSpec(shape, lambda *_: (0,) * nd, pipeline_mode=pl.Buffered(1))


def _hgrn_kernel(*refs, nb_blk, tt, has_s0, n_lb_rows):
    if has_s0:
        (x_ref, s0_ref, g_ref, win_ref, lbl_ref, gn_ref, wout_ref,
         o_ref, sfin_ref, st_ref) = refs
    else:
        (x_ref, g_ref, win_ref, lbl_ref, gn_ref, wout_ref,
         o_ref, sfin_ref, st_ref) = refs
        s0_ref = None
    t = pl.program_id(1)
    m = nb_blk * tt
    ur = min(m, HG_UNIT_ROWS)
    kd = HG_HEADS * HG_DK

    @pl.when(t == 0)
    def _():
        if has_s0:
            for nb in range(nb_blk):
                for h in range(HG_HEADS):
                    st_ref[nb, h] = s0_ref[nb, h].T
        else:
            st_ref[...] = jnp.zeros_like(st_ref)

    ll = lbl_ref[...]
    e = jnp.exp(ll - jnp.max(ll, axis=0, keepdims=True))
    sm = e / jnp.sum(e, axis=0, keepdims=True)
    lb = jnp.sum(sm[:n_lb_rows], axis=0, keepdims=True)

    ri = lax.broadcasted_iota(jnp.int32, (ur, ur), 0)
    ci = lax.broadcasted_iota(jnp.int32, (ur, ur), 1)
    tri = jnp.where(((ri >> 6) == (ci >> 6)) & (ci <= ri), 1.0, 0.0).astype(BF16)
    gn = gn_ref[...]
    t64 = lax.broadcasted_iota(jnp.int32, (CHUNK, CHUNK), 0)
    s64 = lax.broadcasted_iota(jnp.int32, (CHUNK, CHUNK), 1)
    causal = s64 <= t64
    head_lanes = [slice(h * HG_DK, (h + 1) * HG_DK) for h in range(HG_HEADS)]
    x_all = x_ref[...].reshape(m, D_MODEL)

    def project(u0):
        x = x_all[u0:u0 + ur]
        hn = _rms(x, g_ref[...]).astype(BF16)
        proj = _dot(hn, win_ref[...])
        q = proj[:, :kd]
        fpre = proj[:, kd:2 * kd]
        v = proj[:, 2 * kd:3 * kd]
        gate = proj[:, 3 * kd:]
        en = jnp.exp(-jnp.abs(fpre))
        r = 1.0 / (1.0 + en)
        er = en * r
        pos = fpre >= 0.0
        logf = jnp.log(lb + (1.0 - lb) * jnp.where(pos, r, er))
        k = (1.0 - lb) * jnp.where(pos, er, r)
        hi = logf.astype(BF16)
        lo = (logf - hi.astype(F32)).astype(BF16)
        b = _dot(tri, hi) + _dot(tri, lo)
        return x, q, k, v, b, jax.nn.sigmoid(gate)

    units = list(range(0, m, ur))
    projected = [project(u0) for u0 in units]
    sts = {}
    for u0, (x, q, k, v, b, sgate) in zip(units, projected):
        ogs = []
        for c in range(ur // CHUNK):
            nb = (u0 + c * CHUNK) // tt
            if nb not in sts:
                sts[nb] = [st_ref[nb, h] for h in range(HG_HEADS)]
            rows = slice(c * CHUNK, (c + 1) * CHUNK)
            bc = b[rows]
            bl = bc[CHUNK - 1:CHUNK]
            qt = (q[rows] * jnp.exp(bc)).astype(BF16)
            kt = (k[rows] * jnp.exp(-bc)).astype(BF16)
            ke = (k[rows] * jnp.exp(bl - bc)).astype(BF16)
            ebl = jnp.exp(bl)
            vc = v[rows]
            attns = [jnp.where(causal, _dot_nt(qt[:, sl], kt[:, sl]), 0.0) for sl in head_lanes]
            inters = [_dot_nt(qt[:, sl], st.astype(BF16)) for sl, st in zip(head_lanes, sts[nb])]
            upds = [_dot(vc[:, sl].T.astype(BF16), ke[:, sl]) for sl in head_lanes]
            outs = [_dot(attn.astype(BF16), vc[:, sl].astype(BF16)) + inter
                    for attn, sl, inter in zip(attns, head_lanes, inters)]
            sts[nb] = [st * ebl[:, sl] + upd for st, sl, upd in zip(sts[nb], head_lanes, upds)]
            ons = [o * lax.rsqrt(jnp.mean(o * o, axis=-1, keepdims=True) + EPS) for o in outs]
            ogs.append((jnp.concatenate(ons, axis=1) * gn * sgate[rows]).astype(BF16))
        out = x + _dot(jnp.concatenate(ogs, axis=0), wout_ref[...])
        if tt >= ur:
            o_ref[u0 // tt, u0 % tt:u0 % tt + ur, :] = out
        else:
            o_ref[u0 // tt:(u0 + ur) // tt] = out.reshape(ur // tt, tt, D_MODEL)
    for nb, st in sts.items():
        for h in range(HG_HEADS):
            st_ref[nb, h] = st[h]

    @pl.when(t == pl.num_programs(1) - 1)
    def _():
        for nb in range(nb_blk):
            for h in range(HG_HEADS):
                sfin_ref[nb, h] = st_ref[nb, h].T


def _hgrn_layer(x, s0, norm_g, w_in, lb_logits, gn, w_out, layer_idx, nb_blk, tt):
    B, T, D = x.shape
    has_s0 = s0 is not None
    grid = (B // nb_blk, T // tt)
    x_spec = pl.BlockSpec((nb_blk, tt, D), lambda b, t: (b, t, 0))
    st_spec = pl.BlockSpec((nb_blk, HG_HEADS, HG_DK, HG_DK), lambda b, t: (b, 0, 0, 0))
    in_specs = [x_spec]
    args = [x]
    if has_s0:
        in_specs.append(st_spec)
        args.append(s0)
    in_specs += [_const_spec(norm_g.shape), _const_spec(w_in.shape), _const_spec(lb_logits.shape),
                 _const_spec(gn.shape), _const_spec(w_out.shape)]
    args += [norm_g, w_in, lb_logits, gn, w_out]
    return pl.pallas_call(
        functools.partial(_hgrn_kernel, nb_blk=nb_blk, tt=tt, has_s0=has_s0, n_lb_rows=layer_idx + 1),
        grid=grid,
        in_specs=in_specs,
        out_specs=[x_spec, st_spec],
        out_shape=[jax.ShapeDtypeStruct((B, T, D), F32),
                   jax.ShapeDtypeStruct((B, HG_HEADS, HG_DK, HG_DK), F32)],
        scratch_shapes=[pltpu.VMEM((nb_blk, HG_HEADS, HG_DK, HG_DK), F32)],
        compiler_params=pltpu.CompilerParams(
            dimension_semantics=("arbitrary", "arbitrary"), vmem_limit_bytes=VMEM_LIMIT),
        name="hgrn_layer",
    )(*args)


def _ffn_kernel(*refs, nb_blk, tt, has_s0, has_pre, final_norm):
    refs = list(refs)
    x_ref = refs.pop(0)
    att_ref = refs.pop(0) if has_pre else None
    wo_ref = refs.pop(0) if has_pre else None
    s0_ref = refs.pop(0) if has_s0 else None
    g_ref, wup_ref, cw_ref, cb_ref, wdn_ref = refs[:5]
    refs = refs[5:]
    gf_ref = refs.pop(0) if final_norm else None
    y_ref, cst_ref, carry_ref, hmid_ref = refs
    t = pl.program_id(1)
    m = nb_blk * tt

    @pl.when(t == 0)
    def _():
        if has_s0:
            carry_ref[...] = s0_ref[...]
        else:
            carry_ref[...] = jnp.zeros_like(carry_ref)

    x = x_ref[...].reshape(m, D_MODEL)
    if has_pre:
        x = x + _dot(att_ref[...].reshape(m, D_MODEL), wo_ref[...])
    hn = _rms(x, g_ref[...]).astype(BF16)
    up = _dot(hn, wup_ref[...])
    row = lax.broadcasted_iota(jnp.int32, (tt, FF_COLS), 0)
    prev = carry_ref[...]

    for j in range(D_FF // FF_COLS):
        cols = slice(j * FF_COLS, (j + 1) * FF_COLS)
        a = up[:, cols]
        g = up[:, D_FF + j * FF_COLS:D_FF + (j + 1) * FF_COLS]
        w0 = cw_ref[0:1, cols]
        w1 = cw_ref[1:2, cols]
        w2 = cw_ref[2:3, cols]
        bias = cb_ref[:, cols]
        for nb in range(nb_blk):
            rows = slice(nb * tt, (nb + 1) * tt)
            an = a[rows]
            c0 = prev[nb, 0:1, cols]
            c1 = prev[nb, 1:2, cols]
            am1 = jnp.where(row == 0, c1, pltpu.roll(an, 1, axis=0))
            am2 = jnp.where(row == 0, c0, jnp.where(row == 1, c1, pltpu.roll(an, 2, axis=0)))
            carry_ref[nb, :, cols] = an[tt - 2:tt]
            conv = bias + w0 * am2 + w1 * am1 + w2 * an
            hmid_ref[rows, cols] = (jax.nn.gelu(conv) * g[rows]).astype(BF16)

    out = x + _dot(hmid_ref[...], wdn_ref[...])
    if final_norm:
        out = _rms(out, gf_ref[...])
    y_ref[...] = out.reshape(nb_blk, tt, D_MODEL)

    @pl.when(t == pl.num_programs(1) - 1)
    def _():
        cst_ref[...] = carry_ref[...]


def _ffn_layer(x, att, w_o, s0, norm_g, w_up, conv_w, conv_b, w_down, norm_final, layer, nb_blk, tt):
    B, T, D = x.shape
    has_pre = att is not None
    has_s0 = s0 is not None
    final_norm = norm_final is not None
    grid = (B // nb_blk, T // tt)
    x_spec = pl.BlockSpec((nb_blk, tt, D), lambda b, t: (b, t, 0))
    cs_spec = pl.BlockSpec((nb_blk, CONV_WIDTH - 1, D_FF), lambda b, t: (b, 0, 0))
    in_specs = [x_spec]
    args = [x]
    if has_pre:
        in_specs += [x_spec, _const_spec(w_o.shape)]
        args += [att, w_o]
    if has_s0:
        in_specs.append(cs_spec)
        args.append(s0)
    in_specs += [_const_spec(norm_g.shape), _layer_spec(w_up.shape, layer), _const_spec(conv_w.shape),
                 _const_spec(conv_b.shape), _layer_spec(w_down.shape, layer)]
    args += [norm_g, w_up, conv_w, conv_b, w_down]
    if final_norm:
        in_specs.append(_const_spec(norm_final.shape))
        args.append(norm_final)
    return pl.pallas_call(
        functools.partial(_ffn_kernel, nb_blk=nb_blk, tt=tt, has_s0=has_s0, has_pre=has_pre,
                          final_norm=final_norm),
        grid=grid,
        in_specs=in_specs,
        out_specs=[x_spec, cs_spec],
        out_shape=[jax.ShapeDtypeStruct((B, T, D), F32),
                   jax.ShapeDtypeStruct((B, CONV_WIDTH - 1, D_FF), F32)],
        scratch_shapes=[pltpu.VMEM((nb_blk, CONV_WIDTH - 1, D_FF), F32),
                        pltpu.VMEM((nb_blk * tt, D_FF), BF16)],
        compiler_params=pltpu.CompilerParams(
            dimension_semantics=("arbitrary", "arbitrary"), vmem_limit_bytes=VMEM_LIMIT),
        name="conv_ffn",
    )(*args)


def _qkv_kernel(x_ref, g_ref, w_ref, q_ref, k_ref, v_ref, kb_ref, vb_ref, *, nb_blk, tt, feature_major):
    m = nb_blk * tt
    shp = (nb_blk, tt, D_MODEL)
    hn = _rms(x_ref[...].reshape(m, D_MODEL), g_ref[...]).astype(BF16)
    proj = _dot(hn, w_ref[...])
    q_ref[...] = (proj[:, :D_MODEL] * (SB_HEAD_DIM ** -0.5)).astype(BF16).reshape(shp)
    k = proj[:, D_MODEL:2 * D_MODEL]
    v = proj[:, 2 * D_MODEL:]
    if feature_major:
        for nb in range(nb_blk):
            k_ref[nb] = k[nb * tt:(nb + 1) * tt].T
            v_ref[nb] = v[nb * tt:(nb + 1) * tt].T
    else:
        k_ref[...] = k.reshape(shp)
        v_ref[...] = v.reshape(shp)
    kb_ref[...] = k.astype(BF16).reshape(shp)
    vb_ref[...] = v.astype(BF16).reshape(shp)


def _qkv_proj(x, norm_g, w_in, nb_blk, tt, feature_major):
    B, T, D = x.shape
    x_spec = pl.BlockSpec((nb_blk, tt, D), lambda b, t: (b, t, 0))
    kv_spec = pl.BlockSpec((nb_blk, D, tt), lambda b, t: (b, 0, t)) if feature_major else x_spec
    kv_shape = (B, D, T) if feature_major else (B, T, D)
    return pl.pallas_call(
        functools.partial(_qkv_kernel, nb_blk=nb_blk, tt=tt, feature_major=feature_major),
        grid=(B // nb_blk, T // tt),
        in_specs=[x_spec, _const_spec(norm_g.shape), _const_spec(w_in.shape)],
        out_specs=[x_spec, kv_spec, kv_spec, x_spec, x_spec],
        out_shape=[jax.ShapeDtypeStruct((B, T, D), BF16),
                   jax.ShapeDtypeStruct(kv_shape, F32),
                   jax.ShapeDtypeStruct(kv_shape, F32),
                   jax.ShapeDtypeStruct((B, T, D), BF16),
                   jax.ShapeDtypeStruct((B, T, D), BF16)],
        compiler_params=pltpu.CompilerParams(
            dimension_semantics=("arbitrary", "arbitrary"), vmem_limit_bytes=VMEM_LIMIT),
        name="sb_qkv",
    )(x, norm_g, w_in)


def _strict_lower(n):
    j = lax.broadcasted_iota(jnp.int32, (n, n), 0)
    s = lax.broadcasted_iota(jnp.int32, (n, n), 1)
    return jnp.where(j > s, 1.0, 0.0).astype(BF16)


def _split_pair(a):
    lane = lax.broadcasted_iota(jnp.int32, a.shape, 1)
    zero = jnp.zeros_like(a)
    return jnp.where(lane < SB_HEAD_DIM, a, zero), jnp.where(lane >= SB_HEAD_DIM, a, zero)


F32_EXP_ZERO = 104.0


def _any_live(runs):
    m = runs[0]
    for r in runs[1:]:
        m = jnp.minimum(m, r)
    return (jnp.min(m) < F32_EXP_ZERO).astype(jnp.int32)


def _sb_heads_blocks(qs, kss, vss, us, runs, masks, kv_t=False, stack_heads=False):
    zs = [[_dot(q, k) if kv_t else _dot_nt(q, k) for q, k in zip(qs, ks)] for ks in kss]
    zs16 = [[z.astype(BF16) for z in zb] for zb in zs]
    sps = [[jnp.maximum(z, 0.0) + jnp.log(1.0 + jnp.exp(-jnp.abs(z))) for z in zb] for zb in zs16]
    sps = [spb if m is None else [jnp.where(m, sp, 0.0) for sp in spb] for spb, m in zip(sps, masks)]
    if stack_heads:
        n = sps[0][0].shape[0]
        stacked = [_dot(jnp.concatenate(spb, axis=0).astype(BF16), u) for spb, u in zip(sps, us)]
        laters = [[st[h * n:(h + 1) * n] for h in range(len(qs))] for st in stacked]
    else:
        laters = [[_dot(sp.astype(BF16), u) for sp in spb] for spb, u in zip(sps, us)]
    avs = None
    for zb, spb, lb, vs, m in zip(zs, sps, laters, vss, masks):
        ws = [jnp.exp((jnp.minimum(z - sp, 0.0) - later) - run)
              for z, sp, later, run in zip(zb, spb, lb, runs)]
        if m is not None:
            ws = [jnp.where(m, w, 0.0) for w in ws]
        av = [_dot_nt(w.astype(BF16), v) if kv_t else _dot(w.astype(BF16), v) for w, v in zip(ws, vs)]
        avs = av if avs is None else [a + b for a, b in zip(avs, av)]
        runs = [run + later[:, 0:1] + sp[:, 0:1] for run, later, sp in zip(runs, lb, spb)]
    return avs, runs


def _attn_prompt_kernel(q_ref, k_ref, v_ref, o_ref, *, tq, n_lock):
    i = pl.program_id(2)
    lanes = [slice(p * SB_PAIR, (p + 1) * SB_PAIR) for p in range(n_lock)]
    q_all = q_ref[0]
    tiles = (slice(0, tq), slice(tq, 2 * tq))
    q_lo, q_hi = ([qh for sl in lanes for qh in _split_pair(q_all[rows, sl])] for rows in tiles)
    u = _strict_lower(tq)
    row = lax.broadcasted_iota(jnp.int32, (tq, tq), 0)
    col = lax.broadcasted_iota(jnp.int32, (tq, tq), 1)
    causal = col < row
    zero_acc = tuple(jnp.zeros((tq, SB_PAIR), F32) for _ in lanes)
    zero_run = tuple(jnp.zeros((tq, 1), F32) for _ in q_lo)

    def kv(j):
        k0 = pl.multiple_of(j * tq, tq)
        kb = k_ref[0, pl.ds(k0, tq), :]
        vb = v_ref[0, pl.ds(k0, tq), :]
        return ([kb[:, sl] for sl in lanes for _ in range(2)],
                [vh for sl in lanes for vh in _split_pair(vb[:, sl])])

    def visit(qs, blocks, accs, runs, masks):
        avs, runs = _sb_heads_blocks(qs, [b[0] for b in blocks], [b[1] for b in blocks],
                                     [u] * len(blocks), runs, masks)
        accs = tuple(acc + avs[2 * p] + avs[2 * p + 1] for p, acc in enumerate(accs))
        return accs, tuple(runs)

    def finish(qs, rows, j, accs, runs):
        def cond(c):
            j, alive, _, _ = c
            return (j >= 0) & (alive > 0)

        def body(c):
            j, _, accs, runs = c
            accs, runs = visit(qs, [kv(j)], accs, runs, [None])
            return j - 1, _any_live(runs), accs, runs

        _, _, accs, _ = lax.while_loop(cond, body, (j, _any_live(runs), accs, runs))
        o_ref[0, rows, :] = jnp.concatenate(accs, axis=1).astype(BF16)

    @pl.when(i == 0)
    def _():
        own = kv(0)
        lo = visit(q_lo, [own], zero_acc, zero_run, [causal])
        hi = visit(q_hi, [kv(1), own], zero_acc, zero_run, [causal, None])
        for rows, (accs, _) in zip(tiles, (lo, hi)):
            o_ref[0, rows, :] = jnp.concatenate(accs, axis=1).astype(BF16)

    @pl.when(i > 0)
    def _():
        older, mid, newer = kv(2 * i - 1), kv(2 * i), kv(2 * i + 1)
        both = [(hi_b[0] + lo_b[0], hi_b[1] + lo_b[1]) for hi_b, lo_b in ((newer, mid), (mid, older))]
        accs, runs = visit(q_hi + q_lo, both, zero_acc + zero_acc, zero_run + zero_run, [causal, None])
        finish(q_hi, tiles[1], 2 * i - 1, accs[:n_lock], runs[:2 * n_lock])
        finish(q_lo, tiles[0], 2 * i - 2, accs[n_lock:], runs[2 * n_lock:])


def _attn_prompt(q, kb, vb, tq, n_lock):
    B, T, D = q.shape
    width = n_lock * SB_PAIR
    q_spec = pl.BlockSpec((1, 2 * tq, width), lambda b, p, i: (b, i, p))
    kv_spec = pl.BlockSpec((1, T, width), lambda b, p, i: (b, 0, p))
    return pl.pallas_call(
        functools.partial(_attn_prompt_kernel, tq=tq, n_lock=n_lock),
        grid=(B, D // width, T // (2 * tq)),
        in_specs=[q_spec, kv_spec, kv_spec],
        out_specs=q_spec,
        out_shape=jax.ShapeDtypeStruct((B, T, D), BF16),
        compiler_params=pltpu.CompilerParams(
            dimension_semantics=("arbitrary", "arbitrary", "arbitrary"), vmem_limit_bytes=VMEM_LIMIT),
        name="sb_attn_prompt",
    )(q, kb, vb)


def _attn_sample_kernel(q_ref, kn_ref, vn_ref, kc_hbm, vc_hbm, o_ref, kbuf, vbuf, acc_ref, run_ref, sem,
                        *, tq, tk, n_past):
    b = pl.program_id(0)
    j0 = n_past - 1
    own_slot = 2 + lax.rem(b, 2)

    def copies(stream, j, slot):
        k0 = pl.multiple_of(j * tk, tk)
        return (pltpu.make_async_copy(kc_hbm.at[stream, :, :, pl.ds(k0, tk)], kbuf.at[slot], sem.at[0, slot]),
                pltpu.make_async_copy(vc_hbm.at[stream, :, :, pl.ds(k0, tk)], vbuf.at[slot], sem.at[1, slot]))

    def start(stream, j, slot):
        for c in copies(stream, j, slot):
            c.start()

    def wait(j, slot):
        for c in copies(b, j, slot):
            c.wait()

    def slot_of(j):
        return jnp.where(j == j0, own_slot, lax.rem(j0 - j, 2))

    @pl.when(b == 0)
    def _():
        start(b, j0, own_slot)

    @pl.when(b + 1 < pl.num_programs(0))
    def _():
        start(b + 1, j0, 5 - own_slot)

    row = lax.broadcasted_iota(jnp.int32, (tq, tq), 0)
    col = lax.broadcasted_iota(jnp.int32, (tq, tq), 1)
    u_new = _strict_lower(tq)
    q = q_ref[0]
    kn = kn_ref[0]
    vn = vn_ref[0]
    head_lanes = [slice(h * SB_HEAD_DIM, (h + 1) * SB_HEAD_DIM) for h in range(SB_HEADS)]
    q_heads = [q[:, sl] for sl in head_lanes]
    avs, runs = _sb_heads_blocks(q_heads, [[kn[:, sl] for sl in head_lanes]],
                                 [[vn[:, sl] for sl in head_lanes]], [u_new],
                                 [jnp.zeros((tq, 1), F32)] * SB_HEADS, [col < row], stack_heads=True)
    for h in range(SB_HEADS):
        acc_ref[h] = avs[h]
        run_ref[h] = runs[h]
    u = _strict_lower(tk)

    def cond(c):
        j, live = c
        return (j >= 0) & (live > 0)

    def body(c):
        j, _ = c
        slot = slot_of(j)
        wait(j, slot)

        @pl.when(j > 0)
        def _():
            start(b, j - 1, slot_of(j - 1))

        ks = [kbuf[slot, h].astype(BF16) for h in range(SB_HEADS)]
        vs = [vbuf[slot, h].astype(BF16) for h in range(SB_HEADS)]
        avs, runs = _sb_heads_blocks(q_heads, [ks], [vs], [u], [run_ref[h] for h in range(SB_HEADS)],
                                     [None], kv_t=True, stack_heads=True)
        for h in range(SB_HEADS):
            acc_ref[h] += avs[h]
            run_ref[h] = runs[h]
        return j - 1, _any_live(runs)

    jf, _ = lax.while_loop(cond, body, (j0, _any_live(runs)))

    @pl.when(jf >= 0)
    def _():
        wait(jf, slot_of(jf))

    o_ref[0] = jnp.concatenate([acc_ref[h] for h in range(SB_HEADS)], axis=1).astype(BF16)


def _attn_sample(q, kb, vb, k_past, v_past, tk):
    B, T, D = q.shape
    n_past = k_past.shape[3] // tk
    q_spec = pl.BlockSpec((1, T, D), lambda b: (b, 0, 0))
    any_spec = pl.BlockSpec(memory_space=pl.ANY)
    return pl.pallas_call(
        functools.partial(_attn_sample_kernel, tq=T, tk=tk, n_past=n_past),
        grid=(B,),
        in_specs=[q_spec, q_spec, q_spec, any_spec, any_spec],
        out_specs=q_spec,
        out_shape=jax.ShapeDtypeStruct((B, T, D), BF16),
        scratch_shapes=[pltpu.VMEM((4, SB_HEADS, SB_HEAD_DIM, tk), F32),
                        pltpu.VMEM((4, SB_HEADS, SB_HEAD_DIM, tk), F32),
                        pltpu.VMEM((SB_HEADS, T, SB_HEAD_DIM), F32),
                        pltpu.VMEM((SB_HEADS, T, 1), F32),
                        pltpu.SemaphoreType.DMA((2, 4))],
        compiler_params=pltpu.CompilerParams(
            dimension_semantics=("arbitrary",), vmem_limit_bytes=VMEM_LIMIT),
        name="sb_attn_sample",
    )(q, kb, vb, k_past, v_past)


def _step_tile(B, T):
    tt = min(T, ROWS_PER_STEP)
    return max(1, min(B, ROWS_PER_STEP // tt)), tt


def _trunk(x, hg_s0, k_past, v_past, conv_s0, p):
    B, T, D = x.shape
    nb_blk, tt = _step_tile(B, T)
    x1, hg_fin = _hgrn_layer(x, hg_s0, p["norm_mix"][0], p["w_in_a"], p["lb_logits"], p["gn_a"],
                             p["w_out_a"], 0, min(nb_blk, HG_STREAMS_PER_STEP), tt)
    x2, conv0 = _ffn_layer(x1, None, None, None if conv_s0 is None else conv_s0[0],
                           p["norm_ffn"][0], p["w_up"], p["conv_w"][0], p["conv_b"][0],
                           p["w_down"], None, 0, nb_blk, tt)
    feature_major = k_past is None
    q, k, v, kb, vb = _qkv_proj(x2, p["norm_mix"][1], p["w_in_b"], nb_blk, tt, feature_major)
    if k_past is None:
        att = _attn_prompt(q, kb, vb, SB_TILE, SB_LOCKSTEP_PAIRS)
    else:
        att = _attn_sample(q, kb, vb, k_past, v_past, SB_TILE)
    y, conv1 = _ffn_layer(x2, att, p["w_out_b"], None if conv_s0 is None else conv_s0[1],
                          p["norm_ffn"][1], p["w_up"], p["conv_w"][1], p["conv_b"][1],
                          p["w_down"], p["norm_final"], 1, nb_blk, tt)
    if feature_major:
        to_rows = lambda a: jnp.transpose(a.reshape(B, SB_HEADS, SB_HEAD_DIM, T), (0, 3, 1, 2))[None]
    else:
        to_rows = lambda a: a.reshape(B, T, SB_HEADS, SB_HEAD_DIM)[None]
    return (y, hg_fin[None], to_rows(k), to_rows(v), jnp.stack([conv0, conv1]))


def kernel(x_prompt, x_sample, state_hgrn, cache_k, cache_v, state_conv, norm_mix, norm_ffn, norm_final,
           w_in_a, w_out_a, gn_a, lb_logits, w_in_b, w_out_b, w_up, conv_w, conv_b, w_down):
    row = lambda a: a.reshape(a.shape[:-1] + (1, a.shape[-1]))
    p = {
        "norm_mix": row(norm_mix), "norm_ffn": row(norm_ffn), "norm_final": row(norm_final),
        "w_in_a": w_in_a[0].astype(BF16), "w_out_a": w_out_a[0].astype(BF16),
        "gn_a": gn_a[0].reshape(1, HG_HEADS * HG_DK), "lb_logits": lb_logits,
        "w_in_b": w_in_b[0].astype(BF16), "w_out_b": w_out_b[0].astype(BF16),
        "w_up": w_up.astype(BF16), "conv_w": conv_w, "conv_b": row(conv_b),
        "w_down": w_down.astype(BF16),
    }
    yp, hgp, kp, vp, cp = _trunk(x_prompt, None, None, None, None, p)
    ys, hgs, ks, vs, cs = _trunk(
        x_sample, state_hgrn[0], jnp.transpose(cache_k[0], (0, 2, 3, 1)),
        jnp.transpose(cache_v[0], (0, 2, 3, 1)), state_conv, p)
    return (yp, ys, hgp, kp, vp, cp, hgs, ks, vs, cs)
```

```python
import functools

import jax
import jax.numpy as jnp
from jax import lax
from jax.experimental import pallas as pl
from jax.experimental.pallas import tpu as pltpu

F32 = jnp.float32
BF16 = jnp.bfloat16

D_MODEL = 1024
CHUNK = 64
HG_HEADS = 8
HG_DK = 128
HG_UNIT_ROWS = 256
HG_STREAMS_PER_STEP = 4
SB_HEADS = 16
SB_HEAD_DIM = 64
SB_PAIR = 2 * SB_HEAD_DIM
D_FF = 2816
FF_COLS = 256
CONV_WIDTH = 3
EPS = 1e-6
VMEM_LIMIT = 56 * 1024 * 1024
ROWS_PER_STEP = 512
SB_TILE = 256
SB_LOCKSTEP_PAIRS = 4

_NT = (((1,), (1,)), ((), ()))


def _dot(a, b):
    return jnp.dot(a, b, preferred_element_type=F32)


def _dot_nt(a, b):
    return lax.dot_general(a, b, _NT, preferred_element_type=F32)


def _rms(x, g):
    r = lax.rsqrt(jnp.mean(x * x, axis=-1, keepdims=True) + EPS)
    return x * r * g


def _layer_spec(shape, layer):
    nd = len(shape) - 1
    return pl.BlockSpec((None,) + tuple(shape[1:]), lambda *_: (layer,) + (0,) * nd,
                        pipeline_mode=pl.Buffered(1))


def _const_spec(shape):
    nd = len(shape)
    return pl.BlockSpec(shape, lambda *_: (0,) * nd, pipeline_mode=pl.Buffered(1))


def _hgrn_kernel(*refs, nb_blk, tt, has_s0, n_lb_rows):
    if has_s0:
        (x_ref, s0_ref, g_ref, win_ref, lbl_ref, gn_ref, wout_ref,
         o_ref, sfin_ref, st_ref) = refs
    else:
        (x_ref, g_ref, win_ref, lbl_ref, gn_ref, wout_ref,
         o_ref, sfin_ref, st_ref) = refs
        s0_ref = None
    t = pl.program_id(1)
    m = nb_blk * tt
    ur = min(m, HG_UNIT_ROWS)
    kd = HG_HEADS * HG_DK

    @pl.when(t == 0)
    def _():
        if has_s0:
            for nb in range(nb_blk):
                for h in range(HG_HEADS):
                    st_ref[nb, h] = s0_ref[nb, h].T
        else:
            st_ref[...] = jnp.zeros_like(st_ref)

    ll = lbl_ref[...]
    e = jnp.exp(ll - jnp.max(ll, axis=0, keepdims=True))
    sm = e / jnp.sum(e, axis=0, keepdims=True)
    lb = jnp.sum(sm[:n_lb_rows], axis=0, keepdims=True)

    ri = lax.broadcasted_iota(jnp.int32, (ur, ur), 0)
    ci = lax.broadcasted_iota(jnp.int32, (ur, ur), 1)
    tri = jnp.where(((ri >> 6) == (ci >> 6)) & (ci <= ri), 1.0, 0.0).astype(BF16)
    gn = gn_ref[...]
    t64 = lax.broadcasted_iota(jnp.int32, (CHUNK, CHUNK), 0)
    s64 = lax.broadcasted_iota(jnp.int32, (CHUNK, CHUNK), 1)
    causal = s64 <= t64
    head_lanes = [slice(h * HG_DK, (h + 1) * HG_DK) for h in range(HG_HEADS)]
    x_all = x_ref[...].reshape(m, D_MODEL)

    def project(u0):
        x = x_all[u0:u0 + ur]
        hn = _rms(x, g_ref[...]).astype(BF16)
        proj = _dot(hn, win_ref[...])
        q = proj[:, :kd]
        fpre = proj[:, kd:2 * kd]
        v = proj[:, 2 * kd:3 * kd]
        gate = proj[:, 3 * kd:]
        en = jnp.exp(-jnp.abs(fpre))
        r = 1.0 / (1.0 + en)
        er = en * r
        pos = fpre >= 0.0
        logf = jnp.log(lb + (1.0 - lb) * jnp.where(pos, r, er))
        k = (1.0 - lb) * jnp.where(pos, er, r)
        hi = logf.astype(BF16)
        lo = (logf - hi.astype(F32)).astype(BF16)
        b = _dot(tri, hi) + _dot(tri, lo)
        return x, q, k, v, b, jax.nn.sigmoid(gate)

    units = list(range(0, m, ur))
    projected = [project(u0) for u0 in units]
    sts = {}
    for u0, (x, q, k, v, b, sgate) in zip(units, projected):
        ogs = []
        for c in range(ur // CHUNK):
            nb = (u0 + c * CHUNK) // tt
            if nb not in sts:
                sts[nb] = [st_ref[nb, h] for h in range(HG_HEADS)]
            rows = slice(c * CHUNK, (c + 1) * CHUNK)
            bc = b[rows]
            bl = bc[CHUNK - 1:CHUNK]
            qt = (q[rows] * jnp.exp(bc)).astype(BF16)
            kt = (k[rows] * jnp.exp(-bc)).astype(BF16)
            ke = (k[rows] * jnp.exp(bl - bc)).astype(BF16)
            ebl = jnp.exp(bl)
            vc = v[rows]
            attns = [jnp.where(causal, _dot_nt(qt[:, sl], kt[:, sl]), 0.0) for sl in head_lanes]
            inters = [_dot_nt(qt[:, sl], st.astype(BF16)) for sl, st in zip(head_lanes, sts[nb])]
            upds = [_dot(vc[:, sl].T.astype(BF16), ke[:, sl]) for sl in head_lanes]
            outs = [_dot(attn.astype(BF16), vc[:, sl].astype(BF16)) + inter
                    for attn, sl, inter in zip(attns, head_lanes, inters)]
            sts[nb] = [st * ebl[:, sl] + upd for st, sl, upd in zip(sts[nb], head_lanes, upds)]
            ons = [o * lax.rsqrt(jnp.mean(o * o, axis=-1, keepdims=True) + EPS) for o in outs]
            ogs.append((jnp.concatenate(ons, axis=1) * gn * sgate[rows]).astype(BF16))
        out = x + _dot(jnp.concatenate(ogs, axis=0), wout_ref[...])
        if tt >= ur:
            o_ref[u0 // tt, u0 % tt:u0 % tt + ur, :] = out
        else:
            o_ref[u0 // tt:(u0 + ur) // tt] = out.reshape(ur // tt, tt, D_MODEL)
    for nb, st in sts.items():
        for h in range(HG_HEADS):
            st_ref[nb, h] = st[h]

    @pl.when(t == pl.num_programs(1) - 1)
    def _():
        for nb in range(nb_blk):
            for h in range(HG_HEADS):
                sfin_ref[nb, h] = st_ref[nb, h].T


def _hgrn_layer(x, s0, norm_g, w_in, lb_logits, gn, w_out, layer_idx, nb_blk, tt):
    B, T, D = x.shape
    has_s0 = s0 is not None
    grid = (B // nb_blk, T // tt)
    x_spec = pl.BlockSpec((nb_blk, tt, D), lambda b, t: (b, t, 0))
    st_spec = pl.BlockSpec((nb_blk, HG_HEADS, HG_DK, HG_DK), lambda b, t: (b, 0, 0, 0))
    in_specs = [x_spec]
    args = [x]
    if has_s0:
        in_specs.append(st_spec)
        args.append(s0)
    in_specs += [_const_spec(norm_g.shape), _const_spec(w_in.shape), _const_spec(lb_logits.shape),
                 _const_spec(gn.shape), _const_spec(w_out.shape)]
    args += [norm_g, w_in, lb_logits, gn, w_out]
    return pl.pallas_call(
        functools.partial(_hgrn_kernel, nb_blk=nb_blk, tt=tt, has_s0=has_s0, n_lb_rows=layer_idx + 1),
        grid=grid,
        in_specs=in_specs,
        out_specs=[x_spec, st_spec],
        out_shape=[jax.ShapeDtypeStruct((B, T, D), F32),
                   jax.ShapeDtypeStruct((B, HG_HEADS, HG_DK, HG_DK), F32)],
        scratch_shapes=[pltpu.VMEM((nb_blk, HG_HEADS, HG_DK, HG_DK), F32)],
        compiler_params=pltpu.CompilerParams(
            dimension_semantics=("arbitrary", "arbitrary"), vmem_limit_bytes=VMEM_LIMIT),
        name="hgrn_layer",
    )(*args)


def _ffn_kernel(*refs, nb_blk, tt, has_s0, has_pre, final_norm):
    refs = list(refs)
    x_ref = refs.pop(0)
    att_ref = refs.pop(0) if has_pre else None
    wo_ref = refs.pop(0) if has_pre else None
    s0_ref = refs.pop(0) if has_s0 else None
    g_ref, wup_ref, cw_ref, cb_ref, wdn_ref = refs[:5]
    refs = refs[5:]
    gf_ref = refs.pop(0) if final_norm else None
    y_ref, cst_ref, carry_ref, hmid_ref = refs
    t = pl.program_id(1)
    m = nb_blk * tt

    @pl.when(t == 0)
    def _():
        if has_s0:
            carry_ref[...] = s0_ref[...]
        else:
            carry_ref[...] = jnp.zeros_like(carry_ref)

    x = x_ref[...].reshape(m, D_MODEL)
    if has_pre:
        x = x + _dot(att_ref[...].reshape(m, D_MODEL), wo_ref[...])
    hn = _rms(x, g_ref[...]).astype(BF16)
    up = _dot(hn, wup_ref[...])
    row = lax.broadcasted_iota(jnp.int32, (tt, FF_COLS), 0)
    prev = carry_ref[...]

    for j in range(D_FF // FF_COLS):
        cols = slice(j * FF_COLS, (j + 1) * FF_COLS)
        a = up[:, cols]
        g = up[:, D_FF + j * FF_COLS:D_FF + (j + 1) * FF_COLS]
        w0 = cw_ref[0:1, cols]
        w1 = cw_ref[1:2, cols]
        w2 = cw_ref[2:3, cols]
        bias = cb_ref[:, cols]
        for nb in range(nb_blk):
            rows = slice(nb * tt, (nb + 1) * tt)
            an = a[rows]
            c0 = prev[nb, 0:1, cols]
            c1 = prev[nb, 1:2, cols]
            am1 = jnp.where(row == 0, c1, pltpu.roll(an, 1, axis=0))
            am2 = jnp.where(row == 0, c0, jnp.where(row == 1, c1, pltpu.roll(an, 2, axis=0)))
            carry_ref[nb, :, cols] = an[tt - 2:tt]
            conv = bias + w0 * am2 + w1 * am1 + w2 * an
            hmid_ref[rows, cols] = (jax.nn.gelu(conv) * g[rows]).astype(BF16)

    out = x + _dot(hmid_ref[...], wdn_ref[...])
    if final_norm:
        out = _rms(out, gf_ref[...])
    y_ref[...] = out.reshape(nb_blk, tt, D_MODEL)

    @pl.when(t == pl.num_programs(1) - 1)
    def _():
        cst_ref[...] = carry_ref[...]


def _ffn_layer(x, att, w_o, s0, norm_g, w_up, conv_w, conv_b, w_down, norm_final, layer, nb_blk, tt):
    B, T, D = x.shape
    has_pre = att is not None
    has_s0 = s0 is not None
    final_norm = norm_final is not None
    grid = (B // nb_blk, T // tt)
    x_spec = pl.BlockSpec((nb_blk, tt, D), lambda b, t: (b, t, 0))
    cs_spec = pl.BlockSpec((nb_blk, CONV_WIDTH - 1, D_FF), lambda b, t: (b, 0, 0))
    in_specs = [x_spec]
    args = [x]
    if has_pre:
        in_specs += [x_spec, _const_spec(w_o.shape)]
        args += [att, w_o]
    if has_s0:
        in_specs.append(cs_spec)
        args.append(s0)
    in_specs += [_const_spec(norm_g.shape), _layer_spec(w_up.shape, layer), _const_spec(conv_w.shape),
                 _const_spec(conv_b.shape), _layer_spec(w_down.shape, layer)]
    args += [norm_g, w_up, conv_w, conv_b, w_down]
    if final_norm:
        in_specs.append(_const_spec(norm_final.shape))
        args.append(norm_final)
    return pl.pallas_call(
        functools.partial(_ffn_kernel, nb_blk=nb_blk, tt=tt, has_s0=has_s0, has_pre=has_pre,
                          final_norm=final_norm),
        grid=grid,
        in_specs=in_specs,
        out_specs=[x_spec, cs_spec],
        out_shape=[jax.ShapeDtypeStruct((B, T, D), F32),
                   jax.ShapeDtypeStruct((B, CONV_WIDTH - 1, D_FF), F32)],
        scratch_shapes=[pltpu.VMEM((nb_blk, CONV_WIDTH - 1, D_FF), F32),
                        pltpu.VMEM((nb_blk * tt, D_FF), BF16)],
        compiler_params=pltpu.CompilerParams(
            dimension_semantics=("arbitrary", "arbitrary"), vmem_limit_bytes=VMEM_LIMIT),
        name="conv_ffn",
    )(*args)


def _qkv_kernel(x_ref, g_ref, w_ref, q_ref, k_ref, v_ref, kb_ref, vb_ref, *, nb_blk, tt, feature_major):
    m = nb_blk * tt
    shp = (nb_blk, tt, D_MODEL)
    hn = _rms(x_ref[...].reshape(m, D_MODEL), g_ref[...]).astype(BF16)
    proj = _dot(hn, w_ref[...])
    q_ref[...] = (proj[:, :D_MODEL] * (SB_HEAD_DIM ** -0.5)).astype(BF16).reshape(shp)
    k = proj[:, D_MODEL:2 * D_MODEL]
    v = proj[:, 2 * D_MODEL:]
    if feature_major:
        for nb in range(nb_blk):
            k_ref[nb] = k[nb * tt:(nb + 1) * tt].T
            v_ref[nb] = v[nb * tt:(nb + 1) * tt].T
    else:
        k_ref[...] = k.reshape(shp)
        v_ref[...] = v.reshape(shp)
    kb_ref[...] = k.astype(BF16).reshape(shp)
    vb_ref[...] = v.astype(BF16).reshape(shp)


def _qkv_proj(x, norm_g, w_in, nb_blk, tt, feature_major):
    B, T, D = x.shape
    x_spec = pl.BlockSpec((nb_blk, tt, D), lambda b, t: (b, t, 0))
    kv_spec = pl.BlockSpec((nb_blk, D, tt), lambda b, t: (b, 0, t)) if feature_major else x_spec
    kv_shape = (B, D, T) if feature_major else (B, T, D)
    return pl.pallas_call(
        functools.partial(_qkv_kernel, nb_blk=nb_blk, tt=tt, feature_major=feature_major),
        grid=(B // nb_blk, T // tt),
        in_specs=[x_spec, _const_spec(norm_g.shape), _const_spec(w_in.shape)],
        out_specs=[x_spec, kv_spec, kv_spec, x_spec, x_spec],
        out_shape=[jax.ShapeDtypeStruct((B, T, D), BF16),
                   jax.ShapeDtypeStruct(kv_shape, F32),
                   jax.ShapeDtypeStruct(kv_shape, F32),
                   jax.ShapeDtypeStruct((B, T, D), BF16),
                   jax.ShapeDtypeStruct((B, T, D), BF16)],
        compiler_params=pltpu.CompilerParams(
            dimension_semantics=("arbitrary", "arbitrary"), vmem_limit_bytes=VMEM_LIMIT),
        name="sb_qkv",
    )(x, norm_g, w_in)


def _strict_lower(n):
    j = lax.broadcasted_iota(jnp.int32, (n, n), 0)
    s = lax.broadcasted_iota(jnp.int32, (n, n), 1)
    return jnp.where(j > s, 1.0, 0.0).astype(BF16)


def _split_pair(a):
    lane = lax.broadcasted_iota(jnp.int32, a.shape, 1)
    zero = jnp.zeros_like(a)
    return jnp.where(lane < SB_HEAD_DIM, a, zero), jnp.where(lane >= SB_HEAD_DIM, a, zero)


F32_EXP_ZERO = 104.0


def _any_live(runs):
    m = runs[0]
    for r in runs[1:]:
        m = jnp.minimum(m, r)
    return (jnp.min(m) < F32_EXP_ZERO).astype(jnp.int32)


def _sb_heads_blocks(qs, kss, vss, us, runs, masks, kv_t=False, stack_heads=False):
    kv_ts = kv_t if isinstance(kv_t, (list, tuple)) else [kv_t] * len(kss)
    zs = [[_dot(q, k) if t else _dot_nt(q, k) for q, k in zip(qs, ks)] for ks, t in zip(kss, kv_ts)]
    zs16 = [[z.astype(BF16) for z in zb] for zb in zs]
    sps = [[jnp.maximum(z, 0.0) + jnp.log(1.0 + jnp.exp(-jnp.abs(z))) for z in zb] for zb in zs16]
    sps = [spb if m is None else [jnp.where(m, sp, 0.0) for sp in spb] for spb, m in zip(sps, masks)]
    if stack_heads:
        n = sps[0][0].shape[0]
        stacked = [_dot(jnp.concatenate(spb, axis=0).astype(BF16), u) for spb, u in zip(sps, us)]
        laters = [[st[h * n:(h + 1) * n] for h in range(len(qs))] for st in stacked]
    else:
        laters = [[_dot(sp.astype(BF16), u) for sp in spb] for spb, u in zip(sps, us)]
    avs = None
    for zb, spb, lb, vs, m, kv_t in zip(zs, sps, laters, vss, masks, kv_ts):
        ws = [jnp.exp((jnp.minimum(z - sp, 0.0) - later) - run)
              for z, sp, later, run in zip(zb, spb, lb, runs)]
        if m is not None:
            ws = [jnp.where(m, w, 0.0) for w in ws]
        av = [_dot_nt(w.astype(BF16), v) if kv_t else _dot(w.astype(BF16), v) for w, v in zip(ws, vs)]
        avs = av if avs is None else [a + b for a, b in zip(avs, av)]
        runs = [run + later[:, 0:1] + sp[:, 0:1] for run, later, sp in zip(runs, lb, spb)]
    return avs, runs


def _attn_prompt_kernel(q_ref, k_ref, v_ref, o_ref, *, tq, n_lock):
    i = pl.program_id(2)
    lanes = [slice(p * SB_PAIR, (p + 1) * SB_PAIR) for p in range(n_lock)]
    q_all = q_ref[0]
    tiles = (slice(0, tq), slice(tq, 2 * tq))
    q_lo, q_hi = ([qh for sl in lanes for qh in _split_pair(q_all[rows, sl])] for rows in tiles)
    u = _strict_lower(tq)
    row = lax.broadcasted_iota(jnp.int32, (tq, tq), 0)
    col = lax.broadcasted_iota(jnp.int32, (tq, tq), 1)
    causal = col < row
    zero_acc = tuple(jnp.zeros((tq, SB_PAIR), F32) for _ in lanes)
    zero_run = tuple(jnp.zeros((tq, 1), F32) for _ in q_lo)

    def kv(j):
        k0 = pl.multiple_of(j * tq, tq)
        kb = k_ref[0, pl.ds(k0, tq), :]
        vb = v_ref[0, pl.ds(k0, tq), :]
        return ([kb[:, sl] for sl in lanes for _ in range(2)],
                [vh for sl in lanes for vh in _split_pair(vb[:, sl])])

    def visit(qs, blocks, accs, runs, masks):
        avs, runs = _sb_heads_blocks(qs, [b[0] for b in blocks], [b[1] for b in blocks],
                                     [u] * len(blocks), runs, masks)
        accs = tuple(acc + avs[2 * p] + avs[2 * p + 1] for p, acc in enumerate(accs))
        return accs, tuple(runs)

    def finish(qs, rows, j, accs, runs):
        def cond(c):
            j, alive, _, _ = c
            return (j >= 0) & (alive > 0)

        def body(c):
            j, _, accs, runs = c
            accs, runs = visit(qs, [kv(j)], accs, runs, [None])
            return j - 1, _any_live(runs), accs, runs

        _, _, accs, _ = lax.while_loop(cond, body, (j, _any_live(runs), accs, runs))
        o_ref[0, rows, :] = jnp.concatenate(accs, axis=1).astype(BF16)

    @pl.when(i == 0)
    def _():
        own = kv(0)
        lo = visit(q_lo, [own], zero_acc, zero_run, [causal])
        hi = visit(q_hi, [kv(1), own], zero_acc, zero_run, [causal, None])
        for rows, (accs, _) in zip(tiles, (lo, hi)):
            o_ref[0, rows, :] = jnp.concatenate(accs, axis=1).astype(BF16)

    @pl.when(i > 0)
    def _():
        older, mid, newer = kv(2 * i - 1), kv(2 * i), kv(2 * i + 1)
        both = [(hi_b[0] + lo_b[0], hi_b[1] + lo_b[1]) for hi_b, lo_b in ((newer, mid), (mid, older))]
        accs, runs = visit(q_hi + q_lo, both, zero_acc + zero_acc, zero_run + zero_run, [causal, None])
        finish(q_hi, tiles[1], 2 * i - 1, accs[:n_lock], runs[:2 * n_lock])
        finish(q_lo, tiles[0], 2 * i - 2, accs[n_lock:], runs[2 * n_lock:])


def _attn_prompt(q, kb, vb, tq, n_lock):
    B, T, D = q.shape
    width = n_lock * SB_PAIR
    q_spec = pl.BlockSpec((1, 2 * tq, width), lambda b, p, i: (b, i, p))
    kv_spec = pl.BlockSpec((1, T, width), lambda b, p, i: (b, 0, p))
    return pl.pallas_call(
        functools.partial(_attn_prompt_kernel, tq=tq, n_lock=n_lock),
        grid=(B, D // width, T // (2 * tq)),
        in_specs=[q_spec, kv_spec, kv_spec],
        out_specs=q_spec,
        out_shape=jax.ShapeDtypeStruct((B, T, D), BF16),
        compiler_params=pltpu.CompilerParams(
            dimension_semantics=("arbitrary", "arbitrary", "arbitrary"), vmem_limit_bytes=VMEM_LIMIT),
        name="sb_attn_prompt",
    )(q, kb, vb)


def _attn_sample_kernel(q_ref, kn_ref, vn_ref, kc_hbm, vc_hbm, o_ref, kbuf, vbuf, acc_ref, run_ref, sem,
                        *, tq, tk, n_past):
    b = pl.program_id(0)
    j0 = n_past - 1
    own_slot = 2 + lax.rem(b, 2)

    def copies(stream, j, slot):
        k0 = pl.multiple_of(j * tk, tk)
        return (pltpu.make_async_copy(kc_hbm.at[stream, :, :, pl.ds(k0, tk)], kbuf.at[slot], sem.at[0, slot]),
                pltpu.make_async_copy(vc_hbm.at[stream, :, :, pl.ds(k0, tk)], vbuf.at[slot], sem.at[1, slot]))

    def start(stream, j, slot):
        for c in copies(stream, j, slot):
            c.start()

    def wait(j, slot):
        for c in copies(b, j, slot):
            c.wait()

    def slot_of(j):
        return jnp.where(j == j0, own_slot, lax.rem(j0 - j, 2))

    @pl.when(b == 0)
    def _():
        start(b, j0, own_slot)

    @pl.when(b + 1 < pl.num_programs(0))
    def _():
        start(b + 1, j0, 5 - own_slot)

    row = lax.broadcasted_iota(jnp.int32, (tq, tq), 0)
    col = lax.broadcasted_iota(jnp.int32, (tq, tq), 1)
    u_new = _strict_lower(tq)
    q = q_ref[0]
    kn = kn_ref[0]
    vn = vn_ref[0]
    head_lanes = [slice(h * SB_HEAD_DIM, (h + 1) * SB_HEAD_DIM) for h in range(SB_HEADS)]
    q_heads = [q[:, sl] for sl in head_lanes]
    u = _strict_lower(tk)
    wait(j0, own_slot)
    if j0 > 0:
        start(b, j0 - 1, 1)
    ks_past = [kbuf[own_slot, h].astype(BF16) for h in range(SB_HEADS)]
    vs_past = [vbuf[own_slot, h].astype(BF16) for h in range(SB_HEADS)]
    avs, runs = _sb_heads_blocks(q_heads, [[kn[:, sl] for sl in head_lanes], ks_past],
                                 [[vn[:, sl] for sl in head_lanes], vs_past], [u_new, u],
                                 [jnp.zeros((tq, 1), F32)] * SB_HEADS, [col < row, None],
                                 kv_t=[False, True], stack_heads=True)
    for h in range(SB_HEADS):
        acc_ref[h] = avs[h]
        run_ref[h] = runs[h]

    def cond(c):
        j, live = c
        return (j >= 0) & (live > 0)

    def body(c):
        j, _ = c
        slot = slot_of(j)
        wait(j, slot)

        @pl.when(j > 0)
        def _():
            start(b, j - 1, slot_of(j - 1))

        ks = [kbuf[slot, h].astype(BF16) for h in range(SB_HEADS)]
        vs = [vbuf[slot, h].astype(BF16) for h in range(SB_HEADS)]
        avs, runs = _sb_heads_blocks(q_heads, [ks], [vs], [u], [run_ref[h] for h in range(SB_HEADS)],
                                     [None], kv_t=True, stack_heads=True)
        for h in range(SB_HEADS):
            acc_ref[h] += avs[h]
            run_ref[h] = runs[h]
        return j - 1, _any_live(runs)

    jf, _ = lax.while_loop(cond, body, (j0 - 1, _any_live(runs)))

    @pl.when(jf >= 0)
    def _():
        wait(jf, slot_of(jf))

    o_ref[0] = jnp.concatenate([acc_ref[h] for h in range(SB_HEADS)], axis=1).astype(BF16)


def _attn_sample(q, kb, vb, k_past, v_past, tk):
    B, T, D = q.shape
    n_past = k_past.shape[3] // tk
    q_spec = pl.BlockSpec((1, T, D), lambda b: (b, 0, 0))
    any_spec = pl.BlockSpec(memory_space=pl.ANY)
    return pl.pallas_call(
        functools.partial(_attn_sample_kernel, tq=T, tk=tk, n_past=n_past),
        grid=(B,),
        in_specs=[q_spec, q_spec, q_spec, any_spec, any_spec],
        out_specs=q_spec,
        out_shape=jax.ShapeDtypeStruct((B, T, D), BF16),
        scratch_shapes=[pltpu.VMEM((4, SB_HEADS, SB_HEAD_DIM, tk), F32),
                        pltpu.VMEM((4, SB_HEADS, SB_HEAD_DIM, tk), F32),
                        pltpu.VMEM((SB_HEADS, T, SB_HEAD_DIM), F32),
                        pltpu.VMEM((SB_HEADS, T, 1), F32),
                        pltpu.SemaphoreType.DMA((2, 4))],
        compiler_params=pltpu.CompilerParams(
            dimension_semantics=("arbitrary",), vmem_limit_bytes=VMEM_LIMIT),
        name="sb_attn_sample",
    )(q, kb, vb, k_past, v_past)


def _step_tile(B, T):
    tt = min(T, ROWS_PER_STEP)
    return max(1, min(B, ROWS_PER_STEP // tt)), tt


def _trunk(x, hg_s0, k_past, v_past, conv_s0, p):
    B, T, D = x.shape
    nb_blk, tt = _step_tile(B, T)
    x1, hg_fin = _hgrn_layer(x, hg_s0, p["norm_mix"][0], p["w_in_a"], p["lb_logits"], p["gn_a"],
                             p["w_out_a"], 0, min(nb_blk, HG_STREAMS_PER_STEP), tt)
    x2, conv0 = _ffn_layer(x1, None, None, None if conv_s0 is None else conv_s0[0],
                           p["norm_ffn"][0], p["w_up"], p["conv_w"][0], p["conv_b"][0],
                           p["w_down"], None, 0, nb_blk, tt)
    feature_major = k_past is None
    q, k, v, kb, vb = _qkv_proj(x2, p["norm_mix"][1], p["w_in_b"], nb_blk, tt, feature_major)
    if k_past is None:
        att = _attn_prompt(q, kb, vb, SB_TILE, SB_LOCKSTEP_PAIRS)
    else:
        att = _attn_sample(q, kb, vb, k_past, v_past, SB_TILE)
    y, conv1 = _ffn_layer(x2, att, p["w_out_b"], None if conv_s0 is None else conv_s0[1],
                          p["norm_ffn"][1], p["w_up"], p["conv_w"][1], p["conv_b"][1],
                          p["w_down"], p["norm_final"], 1, nb_blk, tt)
    if feature_major:
        to_rows = lambda a: jnp.transpose(a.reshape(B, SB_HEADS, SB_HEAD_DIM, T), (0, 3, 1, 2))[None]
    else:
        to_rows = lambda a: a.reshape(B, T, SB_HEADS, SB_HEAD_DIM)[None]
    return (y, hg_fin[None], to_rows(k), to_rows(v), jnp.stack([conv0, conv1]))


def kernel(x_prompt, x_sample, state_hgrn, cache_k, cache_v, state_conv, norm_mix, norm_ffn, norm_final,
           w_in_a, w_out_a, gn_a, lb_logits, w_in_b, w_out_b, w_up, conv_w, conv_b, w_down):
    row = lambda a: a.reshape(a.shape[:-1] + (1, a.shape[-1]))
    p = {
        "norm_mix": row(norm_mix), "norm_ffn": row(norm_ffn), "norm_final": row(norm_final),
        "w_in_a": w_in_a[0].astype(BF16), "w_out_a": w_out_a[0].astype(BF16),
        "gn_a": gn_a[0].reshape(1, HG_HEADS * HG_DK), "lb_logits": lb_logits,
        "w_in_b": w_in_b[0].astype(BF16), "w_out_b": w_out_b[0].astype(BF16),
        "w_up": w_up.astype(BF16), "conv_w": conv_w, "conv_b": row(conv_b),
        "w_down": w_down.astype(BF16),
    }
    yp, hgp, kp, vp, cp = _trunk(x_prompt, None, None, None, None, p)
    ys, hgs, ks, vs, cs = _trunk(
        x_sample, state_hgrn[0], jnp.transpose(cache_k[0], (0, 2, 3, 1)),
        jnp.transpose(cache_v[0], (0, 2, 3, 1)), state_conv, p)
    return (yp, ys, hgp, kp, vp, cp, hgs, ks, vs, cs)
```
